```python
import jax
import jax.numpy as jnp
from jax import lax
import numpy as np


D_MODEL = 1024
BATCH = 4
SEQ = 4096
DEPTH = 4

GRID_W = 64
CTX_LEN = 256
EPS = 1e-6

CONV_W = D_MODEL
GLA_HEADS = 4
GLA_DK = D_MODEL // 2 // GLA_HEADS
GLA_DV = D_MODEL // GLA_HEADS
GLA_RANK = 16
GLA_NORMALIZER = 16.0
GLA_CHUNK = 64
MLA_HEADS = 8
MLA_NOPE = 128
MLA_ROPE = 64
MLA_DV = 128
MLA_QK = MLA_NOPE + MLA_ROPE
MLA_Q_RANK = 384
MLA_KV_RANK = 128
ROPE_BASE = 10000.0
Q_BLOCK = 128

IN_LAYOUT = (
    ('a_v', CONV_W), ('a_b', CONV_W), ('a_c', CONV_W), ('a_z', CONV_W),
    ('b_q', GLA_HEADS * GLA_DK), ('b_k', GLA_HEADS * GLA_DK), ('b_v', GLA_HEADS * GLA_DV),
    ('b_z', GLA_HEADS * GLA_DV), ('b_af', GLA_RANK), ('b_ab', GLA_RANK),
    ('c_q', MLA_Q_RANK), ('c_kv', MLA_KV_RANK), ('c_kr', MLA_ROPE), ('c_z', MLA_HEADS * MLA_DV),
    ('g_a', D_MODEL), ('g_b', D_MODEL), ('g_c', D_MODEL),
)
IN_DIM = sum(w for _, w in IN_LAYOUT)

kernel_name = 'hybrid_conv_gla_mla_prefix_trunk'


def _rmsnorm(x, g):
    xf = x.astype(jnp.float32)
    y = xf * lax.rsqrt(jnp.mean(xf * xf, axis=-1, keepdims=True) + EPS)
    return (y * g.astype(jnp.float32)).astype(x.dtype)


def _split_in(u):
    out = {}
    start = 0
    for name, width in IN_LAYOUT:
        out[name] = u[..., start:start + width]
        start += width
    return out


def _axial_rope_tables(n_tokens):
    rows = n_tokens // GRID_W
    row = jnp.repeat(jnp.arange(rows, dtype=jnp.float32), GRID_W)
    col = jnp.tile(jnp.arange(GRID_W, dtype=jnp.float32), rows)
    n_freq = MLA_ROPE // 4
    freqs = ROPE_BASE ** (-jnp.arange(n_freq, dtype=jnp.float32) / n_freq)
    ang_r = row[:, None] * freqs[None, :]
    ang_c = col[:, None] * freqs[None, :]
    ang = jnp.concatenate([ang_r, ang_r, ang_c, ang_c], axis=-1)
    return jnp.cos(ang), jnp.sin(ang)


def _apply_axial_rope(x, cos, sin):
    half = MLA_ROPE // 2
    quarter = MLA_ROPE // 4
    xf = x.astype(jnp.float32)

    def rot(blk):
        return jnp.concatenate([-blk[..., quarter:], blk[..., :quarter]], axis=-1)

    rotated = jnp.concatenate([rot(xf[..., :half]), rot(xf[..., half:])], axis=-1)
    return (xf * cos[:, None, :] + rotated * sin[:, None, :]).astype(x.dtype)


def _short_conv(u, w):
    up = jnp.pad(u, ((0, 0), (1, 1), (0, 0)))
    return up[:, :-2] * w[0] + up[:, 1:-1] * w[1] + up[:, 2:] * w[2]


def _short_conv_mixer(u, conv_w):
    y = u['a_b'] * _short_conv(u['a_c'] * u['a_v'], conv_w)
    return y * jax.nn.silu(u['a_z'])


def _gla_log_decay(h_low, w_up, b):
    z = (h_low @ w_up + b).astype(jnp.float32)
    return jax.nn.log_sigmoid(z) / GLA_NORMALIZER


def _gla_query(u):
    bsz, t = u['b_q'].shape[:2]
    return u['b_q'].reshape(bsz, t, GLA_HEADS, GLA_DK) * (GLA_DK ** -0.5)


def _gla_prepare(u, p):
    bsz, t = u['b_k'].shape[:2]
    k = u['b_k'].reshape(bsz, t, GLA_HEADS, GLA_DK)
    v = u['b_v'].reshape(bsz, t, GLA_HEADS, GLA_DV)
    la_f = _gla_log_decay(u['b_af'], p['gla_wa_up_f'], p['gla_ba_f']).reshape(bsz, t, GLA_HEADS, GLA_DK)
    la_b = _gla_log_decay(u['b_ab'], p['gla_wa_up_b'], p['gla_ba_b']).reshape(bsz, t, GLA_HEADS, GLA_DK)
    return k, v, la_f, la_b


def _gla_chunked(q, k, v, loga, s0):
    bsz, t, h, dk = q.shape
    dv = v.shape[-1]
    n = t // GLA_CHUNK
    shp = (bsz, n, GLA_CHUNK, h)
    qc = q.astype(jnp.float32).reshape(shp + (dk,))
    kc = k.astype(jnp.float32).reshape(shp + (dk,))
    vc = v.astype(jnp.float32).reshape(shp + (dv,))
    b = jnp.cumsum(loga.reshape(shp + (dk,)), axis=2)
    b_last = b[:, :, -1:]
    q_dec = qc * jnp.exp(b)
    k_inv = kc * jnp.exp(-b)
    k_end = kc * jnp.exp(b_last - b)
    mask = jnp.tril(jnp.ones((GLA_CHUNK, GLA_CHUNK), dtype=bool))
    att = jnp.where(mask, jnp.einsum('bnihd,bnjhd->bnhij', q_dec, k_inv), 0.0)
    o_intra = jnp.einsum('bnhij,bnjhe->bnihe', att, vc)
    u = jnp.einsum('bnjhd,bnjhe->nbhde', k_end, vc)
    g = jnp.exp(b_last[:, :, 0]).swapaxes(0, 1)

    def step(s, inp):
        g_n, u_n = inp
        return g_n[..., None] * s + u_n, s

    _, s_start = lax.scan(step, s0, (g, u))
    o_inter = jnp.einsum('bnihd,nbhde->bnihe', q_dec, s_start)
    return (o_intra + o_inter).reshape(bsz, t, h, dv).astype(v.dtype)


def _gla_final_state(k, v, loga):
    cum = jnp.cumsum(loga, axis=1)
    w = jnp.exp(cum[:, -1:] - cum)
    return jnp.einsum('bthd,bthe->bhde', k.astype(jnp.float32) * w, v.astype(jnp.float32))


def _gla_bidir(q, k, v, la_f, la_b, s_f, s_b):
    o_f = _gla_chunked(q, k, v, la_f, s_f)
    o_b = jnp.flip(_gla_chunked(jnp.flip(q, 1), jnp.flip(k, 1), jnp.flip(v, 1), jnp.flip(la_b, 1), s_b), 1)
    return o_f + o_b


def _gla_output(o, z, g):
    bsz, t = o.shape[:2]
    o = _rmsnorm(o.astype(z.dtype), g.reshape(GLA_HEADS, GLA_DV))
    return o.reshape(bsz, t, GLA_HEADS * GLA_DV) * jax.nn.silu(z)


def _mla_q(c_q, p, cos, sin):
    bsz, t = c_q.shape[:2]
    q = (_rmsnorm(c_q, p['mla_q_norm_g']) @ p['mla_wq_up']).reshape(bsz, t, MLA_HEADS, MLA_QK)
    q = _rmsnorm(q, p['mla_qn_g'])
    if cos is not None:
        q = jnp.concatenate([q[..., :MLA_NOPE], _apply_axial_rope(q[..., MLA_NOPE:], cos, sin)], axis=-1)
    return q


def _mla_kv(c_kv, c_kr, p, cos, sin):
    bsz, t = c_kv.shape[:2]
    kv = (_rmsnorm(c_kv, p['mla_kv_norm_g']) @ p['mla_wkv_up']).reshape(bsz, t, MLA_HEADS, MLA_NOPE + MLA_DV)
    k_nope, v = kv[..., :MLA_NOPE], kv[..., MLA_NOPE:]
    k_rope = jnp.broadcast_to(c_kr[:, :, None, :], (bsz, t, MLA_HEADS, MLA_ROPE))
    k = _rmsnorm(jnp.concatenate([k_nope, k_rope], axis=-1), p['mla_kn_g'])
    if cos is not None:
        k = jnp.concatenate([k[..., :MLA_NOPE], _apply_axial_rope(k[..., MLA_NOPE:], cos, sin)], axis=-1)
    return k, v


def _attend(q, k, v):
    s = jnp.einsum('bqhd,bkhd->bhqk', q, k).astype(jnp.float32) * (MLA_QK ** -0.5)
    pr = jax.nn.softmax(s, axis=-1).astype(v.dtype)
    return jnp.einsum('bhqk,bkhd->bqhd', pr, v)


def _attend_blocked(q, k, v):
    bsz, t, h, d = q.shape
    nb = t // Q_BLOCK
    qb = q.reshape(bsz, nb, Q_BLOCK, h, d).swapaxes(0, 1)
    ob = lax.map(lambda qi: _attend(qi, k, v), qb)
    return ob.swapaxes(0, 1).reshape(bsz, t, h, v.shape[-1])


def _merge_branches(y_a, y_b, y_c, u, p):
    m = (jax.nn.sigmoid(u['g_a']) * (y_a @ p['w_br_a'])
         + jax.nn.sigmoid(u['g_b']) * (y_b @ p['w_br_b'])
         + jax.nn.sigmoid(u['g_c']) * (y_c @ p['w_br_c']))
    return m @ p['w_out']


def _layer(x, ctx, c, c_ctx, p, cos, sin, update_ctx):
    mod_x = jax.nn.silu(c) @ p['w_mod'] + p['b_mod']
    mod_c = jax.nn.silu(c_ctx) @ p['w_mod'] + p['b_mod']
    shift_x, scale_x, gate_x = jnp.split(mod_x[:, None, :], 3, axis=-1)
    shift_c, scale_c, gate_c = jnp.split(mod_c, 3, axis=-1)
    hx = _rmsnorm(x, p['norm_g']) * (1.0 + scale_x) + shift_x
    hc = _rmsnorm(ctx, p['norm_g']) * (1.0 + scale_c) + shift_c
    ux = _split_in(hx @ p['w_in'])
    uc = _split_in(hc @ p['w_in'])

    kc_g, vc_g, laf_c, lab_c = _gla_prepare(uc, p)
    s_f = _gla_final_state(kc_g, vc_g, laf_c)
    s_b = _gla_final_state(jnp.flip(kc_g, 1), jnp.flip(vc_g, 1), jnp.flip(lab_c, 1))
    k_ctx, v_ctx = _mla_kv(uc['c_kv'], uc['c_kr'], p, None, None)

    bsz, t = x.shape[:2]
    y_a = _short_conv_mixer(ux, p['conv_w'])
    kx, vx, laf_x, lab_x = _gla_prepare(ux, p)
    y_b = _gla_output(_gla_bidir(_gla_query(ux), kx, vx, laf_x, lab_x, s_f, s_b), ux['b_z'], p['gla_norm_g'])
    q_lat = _mla_q(ux['c_q'], p, cos, sin)
    k_lat, v_lat = _mla_kv(ux['c_kv'], ux['c_kr'], p, cos, sin)
    o_c = _attend_blocked(q_lat, jnp.concatenate([k_lat, k_ctx], axis=1), jnp.concatenate([v_lat, v_ctx], axis=1))
    y_c = o_c.reshape(bsz, t, MLA_HEADS * MLA_DV) * jax.nn.silu(ux['c_z'])
    x_new = x + gate_x * _merge_branches(y_a, y_b, y_c, ux, p)

    if update_ctx:
        lc = ctx.shape[1]
        zeros = jnp.zeros_like(s_f)
        yc_a = _short_conv_mixer(uc, p['conv_w'])
        yc_b = _gla_output(_gla_bidir(_gla_query(uc), kc_g, vc_g, laf_c, lab_c, zeros, zeros), uc['b_z'], p['gla_norm_g'])
        q_ctx = _mla_q(uc['c_q'], p, None, None)
        yc_c = _attend(q_ctx, k_ctx, v_ctx).reshape(ctx.shape[0], lc, MLA_HEADS * MLA_DV) * jax.nn.silu(uc['c_z'])
        ctx = ctx + gate_c * _merge_branches(yc_a, yc_b, yc_c, uc, p)
    return x_new, ctx


def setup_inputs(seed: int = 0) -> dict:
    key = jax.random.key(seed)
    ks = jax.random.split(key, 24)
    f32 = jnp.float32

    def nrm(k, shape, s):
        return jax.random.normal(k, shape, f32) * s

    def gain(k, shape):
        return 1.0 + 0.02 * jax.random.normal(k, shape, f32)

    L, D = DEPTH, D_MODEL
    return {
        'x': nrm(ks[0], (BATCH, SEQ, D), 1.0),
        'c': nrm(ks[1], (BATCH, D), 1.0),
        'ctx': nrm(ks[2], (BATCH, CTX_LEN, D), 1.0),
        'c_ctx': nrm(ks[3], (D,), 1.0),
        'w_mod': nrm(ks[4], (L, D, 3 * D), 0.5 * D ** -0.5),
        'b_mod': nrm(ks[5], (L, 3 * D), 0.02),
        'norm_g': gain(ks[6], (L, D)),
        'w_in': nrm(ks[7], (L, D, IN_DIM), D ** -0.5),
        'conv_w': nrm(ks[8], (L, 3, CONV_W), 3 ** -0.5),
        'gla_wa_up_f': nrm(ks[9], (L, GLA_RANK, GLA_HEADS * GLA_DK), GLA_RANK ** -0.5),
        'gla_ba_f': nrm(ks[10], (L, GLA_HEADS * GLA_DK), 0.1),
        'gla_wa_up_b': nrm(ks[11], (L, GLA_RANK, GLA_HEADS * GLA_DK), GLA_RANK ** -0.5),
        'gla_ba_b': nrm(ks[12], (L, GLA_HEADS * GLA_DK), 0.1),
        'gla_norm_g': gain(ks[13], (L, GLA_HEADS * GLA_DV)),
        'mla_q_norm_g': gain(ks[14], (L, MLA_Q_RANK)),
        'mla_kv_norm_g': gain(ks[15], (L, MLA_KV_RANK)),
        'mla_wq_up': nrm(ks[16], (L, MLA_Q_RANK, MLA_HEADS * MLA_QK), MLA_Q_RANK ** -0.5),
        'mla_wkv_up': nrm(ks[17], (L, MLA_KV_RANK, MLA_HEADS * (MLA_NOPE + MLA_DV)), MLA_KV_RANK ** -0.5),
        'mla_qn_g': gain(ks[18], (L, MLA_QK)),
        'mla_kn_g': gain(ks[19], (L, MLA_QK)),
        'w_br_a': nrm(ks[20], (L, CONV_W, D), CONV_W ** -0.5),
        'w_br_b': nrm(ks[21], (L, GLA_HEADS * GLA_DV, D), (GLA_HEADS * GLA_DV) ** -0.5),
        'w_br_c': nrm(ks[22], (L, MLA_HEADS * MLA_DV, D), (MLA_HEADS * MLA_DV) ** -0.5),
        'w_out': nrm(ks[23], (L, D, D), D ** -0.5),
    }


def reference(x, c, ctx, c_ctx, w_mod, b_mod, norm_g, w_in, conv_w, gla_wa_up_f, gla_ba_f, gla_wa_up_b,
              gla_ba_b, gla_norm_g, mla_q_norm_g, mla_kv_norm_g, mla_wq_up, mla_wkv_up, mla_qn_g, mla_kn_g,
              w_br_a, w_br_b, w_br_c, w_out):
    cos, sin = _axial_rope_tables(x.shape[1])
    for l in range(DEPTH):
        p = {
            'w_mod': w_mod[l], 'b_mod': b_mod[l], 'norm_g': norm_g[l], 'w_in': w_in[l], 'conv_w': conv_w[l],
            'gla_wa_up_f': gla_wa_up_f[l], 'gla_ba_f': gla_ba_f[l], 'gla_wa_up_b': gla_wa_up_b[l],
            'gla_ba_b': gla_ba_b[l], 'gla_norm_g': gla_norm_g[l], 'mla_q_norm_g': mla_q_norm_g[l],
            'mla_kv_norm_g': mla_kv_norm_g[l], 'mla_wq_up': mla_wq_up[l], 'mla_wkv_up': mla_wkv_up[l],
            'mla_qn_g': mla_qn_g[l], 'mla_kn_g': mla_kn_g[l], 'w_br_a': w_br_a[l], 'w_br_b': w_br_b[l],
            'w_br_c': w_br_c[l], 'w_out': w_out[l],
        }
        x, ctx = _layer(x, ctx, c, c_ctx, p, cos, sin, l < DEPTH - 1)
    return x
```

```python
import functools
import math

import numpy as np
import jax
import jax.numpy as jnp
from jax import lax
from jax.experimental import pallas as pl
from jax.experimental.pallas import tpu as pltpu

D = 1024
BATCH = 4
SEQ = 4096
DEPTH = 4
GRID_W = 64
CTX = 256
EPS = 1e-6

GLA_H = 4
GLA_DK = 128
GLA_DV = 256
GLA_RANK = 16
GLA_TAU = 16.0
CHUNK = 64

MLA_H = 8
MLA_NOPE = 128
MLA_ROPE = 64
MLA_DV = 128
MLA_QK = MLA_NOPE + MLA_ROPE
MLA_QPAD = 256
MLA_Q_RANK = 384
MLA_KV_RANK = 128
ROPE_BASE = 10000.0

SB = SEQ + CTX
N_ROWS = BATCH * SB
ROW_TILE = 256
TILES_PER_BATCH = SB // ROW_TILE
X_TILES = SEQ // ROW_TILE
N_TILES = N_ROWS // ROW_TILE
CTX_GROUP = BATCH

IN_PAD = 12288
COL_AV, COL_AB, COL_AC, COL_AZ, COL_BV, COL_BZ, COL_CZ, COL_GA, COL_GB, COL_GC = range(10)
COL_BQ, COL_BK, COL_CQKV = 20, 21, 22
COL_SMALL = 92
SMALL_KR, SMALL_AF, SMALL_AB = 0, 64, 80

IN_TN = 2048
Q_TILE = 512
KV_TILE = 1024

VMEM_LIMIT = 56 * 1024 * 1024

F32 = jnp.float32
BF16 = jnp.bfloat16


def _dot(a, b):
    return jnp.dot(a, b, preferred_element_type=F32)


def _dot_nt(a, b):
    return lax.dot_general(a, b, (((1,), (1,)), ((), ())), preferred_element_type=F32)


def _dot_exact(a, b):
    return jnp.dot(a, b, preferred_element_type=F32, precision=lax.Precision.HIGHEST)


def _sigmoid(x):
    return 1.0 / (1.0 + jnp.exp(-x))


def _silu(x):
    return x * _sigmoid(x)


def _tile_group(i):
    tt = i % TILES_PER_BATCH
    return jnp.where(tt >= X_TILES, CTX_GROUP, i // TILES_PER_BATCH)


def _mod_kernel(cc_ref, w_ref, b_ref, o_ref):
    o_ref[0] = _dot_exact(_silu(cc_ref[...]), w_ref[0]) + b_ref[0]


def _modulation(cc, w_mod, b_mod):
    return pl.pallas_call(
        _mod_kernel,
        grid=(DEPTH,),
        in_specs=[
            pl.BlockSpec((8, D), lambda l: (0, 0)),
            pl.BlockSpec((1, D, 3 * D), lambda l: (l, 0, 0)),
            pl.BlockSpec((1, 1, 3 * D), lambda l: (l, 0, 0)),
        ],
        out_specs=pl.BlockSpec((1, 8, 3 * D), lambda l: (l, 0, 0)),
        out_shape=jax.ShapeDtypeStruct((DEPTH, 8, 3 * D), F32),
        compiler_params=pltpu.CompilerParams(
            dimension_semantics=("arbitrary",), vmem_limit_bytes=VMEM_LIMIT),
        name="modulation",
    )(cc, w_mod, b_mod.reshape(DEPTH, 1, 3 * D))


def _inproj_kernel(x_ref, mod_ref, g_ref, w_ref, u_ref):
    x = x_ref[...]
    y = x * lax.rsqrt(jnp.mean(x * x, axis=-1, keepdims=True) + EPS) * g_ref[...]
    mod = mod_ref[0]
    h = y * (1.0 + mod[:, D:2 * D]) + mod[:, 0:D]
    u_ref[...] = _dot(h.astype(BF16), w_ref[...])


def _inproj(xs, mod_l, norm_g, w_in_p):
    return pl.pallas_call(
        _inproj_kernel,
        grid=(IN_PAD // IN_TN, N_TILES),
        in_specs=[
            pl.BlockSpec((ROW_TILE, D), lambda j, i: (i, 0)),
            pl.BlockSpec((1, 1, 3 * D), lambda j, i: (_tile_group(i), 0, 0)),
            pl.BlockSpec((1, D), lambda j, i: (0, 0)),
            pl.BlockSpec((D, IN_TN), lambda j, i: (0, j)),
        ],
        out_specs=pl.BlockSpec((ROW_TILE, IN_TN), lambda j, i: (i, j)),
        out_shape=jax.ShapeDtypeStruct((N_ROWS, IN_PAD), F32),
        compiler_params=pltpu.CompilerParams(
            dimension_semantics=("arbitrary", "arbitrary"), vmem_limit_bytes=VMEM_LIMIT),
        name="inproj",
    )(xs, mod_l, norm_g, w_in_p)


def _swap16(r):
    lane = lax.broadcasted_iota(jnp.int32, r.shape, 1)
    even = ((lane // 16) % 2) == 0
    return jnp.where(even, pltpu.roll(r, 112, axis=1), pltpu.roll(r, 16, axis=1))


def _mla_prep_kernel(cqkv_ref, small_ref, cos_ref, sin_ref, wq_ref, wkv_ref, gq_ref, gkv_ref,
                     qn_ref, kn_ref, q_ref, k_ref, v_ref):
    cqkv = cqkv_ref[...]
    cq = cqkv[:, :MLA_Q_RANK]
    ckv = cqkv[:, MLA_Q_RANK:]
    cos = cos_ref[...]
    sin = sin_ref[...]

    def rope(r):
        return r * cos + _swap16(r) * sin

    cq_n = cq * lax.rsqrt(jnp.mean(cq * cq, axis=-1, keepdims=True) + EPS) * gq_ref[...]
    q_all = _dot(cq_n.astype(BF16), wq_ref[...])
    ckv_n = ckv * lax.rsqrt(jnp.mean(ckv * ckv, axis=-1, keepdims=True) + EPS) * gkv_ref[...]
    kv_all = _dot(ckv_n.astype(BF16), wkv_ref[...])

    small = small_ref[...]
    lane = lax.broadcasted_iota(jnp.int32, small.shape, 1)
    kr = jnp.where(lane < MLA_ROPE, small, 0.0)
    kr_ss = jnp.sum(kr * kr, axis=-1, keepdims=True)
    qn_g = qn_ref[...]
    kn_g = kn_ref[...]

    for h in range(MLA_H):
        qh = q_all[:, h * MLA_QPAD:(h + 1) * MLA_QPAD]
        inv = lax.rsqrt(jnp.sum(qh * qh, axis=-1, keepdims=True) * (1.0 / MLA_QK) + EPS)
        qh = qh * inv * qn_g
        q_ref[h, :, 0:MLA_NOPE] = qh[:, 0:MLA_NOPE].astype(BF16)
        q_ref[h, :, MLA_NOPE:MLA_QPAD] = rope(qh[:, MLA_NOPE:MLA_QPAD]).astype(BF16)

        kn = kv_all[:, h * 256:h * 256 + MLA_NOPE]
        inv = lax.rsqrt((jnp.sum(kn * kn, axis=-1, keepdims=True) + kr_ss) * (1.0 / MLA_QK) + EPS)
        k_ref[h, :, 0:MLA_NOPE] = (kn * inv * kn_g[:, 0:MLA_NOPE]).astype(BF16)
        k_ref[h, :, MLA_NOPE:MLA_QPAD] = rope(kr * inv * kn_g[:, MLA_NOPE:MLA_QPAD]).astype(BF16)
        v_ref[h] = kv_all[:, h * 256 + MLA_NOPE:(h + 1) * 256].astype(BF16)


def _mla_prep(u, cos_t, sin_t, wq_p, wkv, gq, gkv, qn_p, kn_p):
    const = lambda i: (0, 0)
    return pl.pallas_call(
        _mla_prep_kernel,
        grid=(N_TILES,),
        in_specs=[
            pl.BlockSpec((ROW_TILE, 512), lambda i: (i, COL_CQKV)),
            pl.BlockSpec((ROW_TILE, 128), lambda i: (i, COL_SMALL)),
            pl.BlockSpec((ROW_TILE, 128), lambda i: (i % TILES_PER_BATCH, 0)),
            pl.BlockSpec((ROW_TILE, 128), lambda i: (i % TILES_PER_BATCH, 0)),
            pl.BlockSpec((MLA_Q_RANK, MLA_H * MLA_QPAD), const),
            pl.BlockSpec((MLA_KV_RANK, MLA_H * 256), const),
            pl.BlockSpec((1, MLA_Q_RANK), const),
            pl.BlockSpec((1, MLA_KV_RANK), const),
            pl.BlockSpec((1, MLA_QPAD), const),
            pl.BlockSpec((1, MLA_QPAD), const),
        ],
        out_specs=[
            pl.BlockSpec((MLA_H, ROW_TILE, MLA_QPAD), lambda i: (0, i, 0)),
            pl.BlockSpec((MLA_H, ROW_TILE, MLA_QPAD), lambda i: (0, i, 0)),
            pl.BlockSpec((MLA_H, ROW_TILE, MLA_DV), lambda i: (0, i, 0)),
        ],
        out_shape=[
            jax.ShapeDtypeStruct((MLA_H, N_ROWS, MLA_QPAD), BF16),
            jax.ShapeDtypeStruct((MLA_H, N_ROWS, MLA_QPAD), BF16),
            jax.ShapeDtypeStruct((MLA_H, N_ROWS, MLA_DV), BF16),
        ],
        compiler_params=pltpu.CompilerParams(
            dimension_semantics=("arbitrary",), vmem_limit_bytes=VMEM_LIMIT),
        name="mla_prep",
    )(u, u, cos_t, sin_t, wq_p, wkv, gq, gkv, qn_p, kn_p)


_SM_SCALE_LOG2E = (MLA_QK ** -0.5) * math.log2(math.e)


def _attn_kernel(q_ref, k_ref, v_ref, o_ref):
    def softmax_tile(q, kv_bounds):
        m = l = acc = None
        for (lo, hi) in kv_bounds:
            s = _dot_nt(q, k_ref[0, lo:hi, :])
            m_c = jnp.max(s, axis=-1, keepdims=True)
            m_new = m_c if m is None else jnp.maximum(m, m_c)
            p = jnp.exp2((s - m_new) * _SM_SCALE_LOG2E)
            l_c = jnp.sum(p, axis=-1, keepdims=True)
            pv = _dot(p.astype(BF16), v_ref[0, lo:hi, :])
            if m is None:
                l, acc = l_c, pv
            else:
                alpha = jnp.exp2((m - m_new) * _SM_SCALE_LOG2E)
                l = alpha * l + l_c
                acc = alpha * acc + pv
            m = m_new
        return acc / l

    lat_bounds = [(c * KV_TILE, (c + 1) * KV_TILE) for c in range(SEQ // KV_TILE)] + [(SEQ, SB)]

    def q_step(i, carry):
        r0 = pl.multiple_of(i * Q_TILE, Q_TILE)
        o_ref[pl.ds(r0, Q_TILE), :] = softmax_tile(q_ref[0, pl.ds(r0, Q_TILE), :], lat_bounds)
        return carry

    lax.fori_loop(0, SEQ // Q_TILE, q_step, 0)
    o_ref[SEQ:SB, :] = softmax_tile(q_ref[0, SEQ:SB, :], [(SEQ, SB)])


def _attention(q, k, v):
    return pl.pallas_call(
        _attn_kernel,
        grid=(BATCH, MLA_H),
        in_specs=[
            pl.BlockSpec((1, SB, MLA_QPAD), lambda b, h: (h, b, 0)),
            pl.BlockSpec((1, SB, MLA_QPAD), lambda b, h: (h, b, 0)),
            pl.BlockSpec((1, SB, MLA_DV), lambda b, h: (h, b, 0)),
        ],
        out_specs=pl.BlockSpec((SB, MLA_DV), lambda b, h: (b, h)),
        out_shape=jax.ShapeDtypeStruct((N_ROWS, MLA_H * MLA_DV), F32),
        compiler_params=pltpu.CompilerParams(
            dimension_semantics=("arbitrary", "arbitrary"), vmem_limit_bytes=VMEM_LIMIT),
        name="mla_attention",
    )(q, k, v)


def _log_sigmoid(z):
    return jnp.minimum(z, 0.0) - jnp.log1p(jnp.exp(-jnp.abs(z)))


def _gla_kernel(qf_ref, kf_ref, vf_ref, sf_ref, qb_ref, kb_ref, vb_ref, sb_ref, wup_ref, bup_ref,
                of_ref, ob_ref, st_ref):
    @pl.when(pl.program_id(1) == 0)
    def _():
        st_ref[...] = jnp.zeros_like(st_ref)

    hk = GLA_H * GLA_DK
    row = lax.broadcasted_iota(jnp.int32, (CHUNK, CHUNK), 0)
    col = lax.broadcasted_iota(jnp.int32, (CHUNK, CHUNK), 1)
    q_scale = GLA_DK ** -0.5

    def direction(d, q_ref, k_ref, v_ref, s_ref, o_ref):
        keep = (row >= col) if d == 0 else (row <= col)
        tri = keep.astype(F32)
        z = _dot(s_ref[...].astype(BF16), wup_ref[:, d * hk:(d + 1) * hk]) + bup_ref[:, d * hk:(d + 1) * hk]
        la = _log_sigmoid(z) * (1.0 / GLA_TAU)
        n_chunks = ROW_TILE // CHUNK
        order = range(n_chunks) if d == 0 else range(n_chunks - 1, -1, -1)
        for c in order:
            r0, r1 = c * CHUNK, (c + 1) * CHUNK
            b = _dot_exact(tri, la[r0:r1])
            tot = b[CHUNK - 1:CHUNK] if d == 0 else b[0:1]
            q_dec = (q_ref[r0:r1, :] * q_scale) * jnp.exp(b)
            kc = k_ref[r0:r1, :]
            k_inv = (kc * jnp.exp(-b)).astype(BF16)
            k_end = (kc * jnp.exp(tot - b)).astype(BF16)
            g = jnp.exp(tot)
            q_dec = q_dec.astype(BF16)
            for h in range(GLA_H):
                ks = slice(h * GLA_DK, (h + 1) * GLA_DK)
                vs = slice(h * GLA_DV, (h + 1) * GLA_DV)
                vh = v_ref[r0:r1, vs]
                vh_b = vh.astype(BF16)
                att = jnp.where(keep, _dot_nt(q_dec[:, ks], k_inv[:, ks]), 0.0)
                st = st_ref[d, h]
                o = _dot(att.astype(BF16), vh_b) + _dot_nt(q_dec[:, ks], st.astype(BF16))
                o_ref[r0:r1, vs] = o
                upd = _dot(vh.T.astype(BF16), k_end[:, ks])
                st_ref[d, h] = st * g[:, ks] + upd

    direction(0, qf_ref, kf_ref, vf_ref, sf_ref, of_ref)
    direction(1, qb_ref, kb_ref, vb_ref, sb_ref, ob_ref)


def _gla(u, wup_p, bup_p):
    def fwd_tile(b, s):
        return b * TILES_PER_BATCH + jnp.where(s == 0, X_TILES, s - 1)

    def bwd_tile(b, s):
        return b * TILES_PER_BATCH + jnp.where(s == 0, X_TILES, X_TILES - s)

    def specs(tile):
        return [
            pl.BlockSpec((ROW_TILE, 512), lambda b, s: (tile(b, s), COL_BQ)),
            pl.BlockSpec((ROW_TILE, 512), lambda b, s: (tile(b, s), COL_BK)),
            pl.BlockSpec((ROW_TILE, 1024), lambda b, s: (tile(b, s), COL_BV)),
            pl.BlockSpec((ROW_TILE, 128), lambda b, s: (tile(b, s), COL_SMALL)),
        ]

    return pl.pallas_call(
        _gla_kernel,
        grid=(BATCH, TILES_PER_BATCH),
        in_specs=specs(fwd_tile) + specs(bwd_tile) + [
            pl.BlockSpec((128, 2 * GLA_H * GLA_DK), lambda b, s: (0, 0)),
            pl.BlockSpec((1, 2 * GLA_H * GLA_DK), lambda b, s: (0, 0)),
        ],
        out_specs=[
            pl.BlockSpec((ROW_TILE, GLA_H * GLA_DV), lambda b, s: (fwd_tile(b, s), 0)),
            pl.BlockSpec((ROW_TILE, GLA_H * GLA_DV), lambda b, s: (bwd_tile(b, s), 0)),
        ],
        out_shape=[jax.ShapeDtypeStruct((N_ROWS, GLA_H * GLA_DV), F32)] * 2,
        scratch_shapes=[pltpu.VMEM((2, GLA_H, GLA_DV, GLA_DK), F32)],
        compiler_params=pltpu.CompilerParams(
            dimension_semantics=("arbitrary", "arbitrary"), vmem_limit_bytes=VMEM_LIMIT),
        name="gla_bidir",
    )(u, u, u, u, u, u, u, u, wup_p, bup_p)


def _merge_kernel(av_ref, ab_ref, ac_ref, az_ref, bz_ref, cz_ref, ga_ref, gb_ref, gc_ref,
                  avp_ref, acp_ref, avn_ref, acn_ref, of_ref, ob_ref, oc_ref, x_ref, mod_ref,
                  cw_ref, gng_ref, wa_ref, wb_ref, wc_ref, wo_ref, out_ref, *, tiles_per_batch):
    tt = pl.program_id(0) % tiles_per_batch
    prev_ok = jnp.logical_and(tt != 0, tt < X_TILES).astype(F32)
    next_ok = (tt < X_TILES - 1).astype(F32)

    p = ac_ref[...] * av_ref[...]
    p_prev = acp_ref[7:8, :] * avp_ref[7:8, :] * prev_ok
    p_next = acn_ref[0:1, :] * avn_ref[0:1, :] * next_ok
    rows = lax.broadcasted_iota(jnp.int32, p.shape, 0)
    p_up = jnp.where(rows == 0, p_prev, pltpu.roll(p, 1, axis=0))
    p_dn = jnp.where(rows == ROW_TILE - 1, p_next, pltpu.roll(p, ROW_TILE - 1, axis=0))
    cw = cw_ref[...]
    conv = p_up * cw[0:1] + p * cw[1:2] + p_dn * cw[2:3]
    y_a = ab_ref[...] * conv * _silu(az_ref[...])
    m = _sigmoid(ga_ref[...]) * _dot(y_a.astype(BF16), wa_ref[...])

    o = of_ref[...] + ob_ref[...]
    gng = gng_ref[...]
    y_b = []
    for h in range(GLA_H):
        sl = slice(h * GLA_DV, (h + 1) * GLA_DV)
        oh = o[:, sl]
        y_b.append(oh * lax.rsqrt(jnp.mean(oh * oh, axis=-1, keepdims=True) + EPS) * gng[:, sl])
    y_b = jnp.concatenate(y_b, axis=-1) * _silu(bz_ref[...])
    m = m + _sigmoid(gb_ref[...]) * _dot(y_b.astype(BF16), wb_ref[...])

    y_c = oc_ref[...] * _silu(cz_ref[...])
    m = m + _sigmoid(gc_ref[...]) * _dot(y_c.astype(BF16), wc_ref[...])

    gate = mod_ref[0][:, 2 * D:3 * D]
    out_ref[...] = x_ref[...] + gate * _dot(m.astype(BF16), wo_ref[...])


def _merge(u, o_f, o_b, o_c, xs, mod_l, conv_w, gla_norm_g, wa, wb, wc, wo, last):
    halo_blocks = ROW_TILE // 8
    n_halo = N_ROWS // 8
    if last:
        grid = (BATCH * X_TILES,)
        tile = lambda i: (i // X_TILES) * TILES_PER_BATCH + i % X_TILES
        out_rows = BATCH * SEQ
        out_map = lambda i: (i, 0)
    else:
        grid = (N_TILES,)
        tile = lambda i: i
        out_rows = N_ROWS
        out_map = lambda i: (i, 0)

    def col(c):
        return pl.BlockSpec((ROW_TILE, D), lambda i: (tile(i), c))

    def prev_rows(c):
        return pl.BlockSpec((8, D), lambda i: (jnp.maximum(tile(i) * halo_blocks - 1, 0), c))

    def next_rows(c):
        return pl.BlockSpec((8, D), lambda i: (jnp.minimum((tile(i) + 1) * halo_blocks, n_halo - 1), c))

    row_block = pl.BlockSpec((ROW_TILE, D), lambda i: (tile(i), 0))
    const = lambda i: (0, 0)
    weight = pl.BlockSpec((D, D), const)

    return pl.pallas_call(
        functools.partial(_merge_kernel, tiles_per_batch=X_TILES if last else TILES_PER_BATCH),
        grid=grid,
        in_specs=[col(COL_AV), col(COL_AB), col(COL_AC), col(COL_AZ), col(COL_BZ), col(COL_CZ),
                  col(COL_GA), col(COL_GB), col(COL_GC),
                  prev_rows(COL_AV), prev_rows(COL_AC), next_rows(COL_AV), next_rows(COL_AC),
                  row_block, row_block, row_block, row_block,
                  pl.BlockSpec((1, 1, 3 * D), lambda i: (_tile_group(tile(i)), 0, 0)),
                  pl.BlockSpec((3, D), const), pl.BlockSpec((1, D), const),
                  weight, weight, weight, weight],
        out_specs=pl.BlockSpec((ROW_TILE, D), out_map),
        out_shape=jax.ShapeDtypeStruct((out_rows, D), F32),
        compiler_params=pltpu.CompilerParams(
            dimension_semantics=("arbitrary",), vmem_limit_bytes=VMEM_LIMIT),
        name="merge_last" if last else "merge",
    )(u, u, u, u, u, u, u, u, u, u, u, u, u, o_f, o_b, o_c, xs, mod_l, conv_w, gla_norm_g,
      wa, wb, wc, wo)


def _rope_tables():
    t = np.arange(SEQ)
    row = (t // GRID_W).astype(np.float32)
    colp = (t % GRID_W).astype(np.float32)
    n_freq = MLA_ROPE // 4
    freqs = (np.float32(ROPE_BASE) ** (-np.arange(n_freq, dtype=np.float32) / np.float32(n_freq))).astype(np.float32)
    ang_r = row[:, None] * freqs[None, :]
    ang_c = colp[:, None] * freqs[None, :]
    ang = np.concatenate([ang_r, ang_r, ang_c, ang_c], axis=-1).astype(np.float32)
    cos = np.ones((SB, 128), np.float32)
    sin = np.zeros((SB, 128), np.float32)
    cos[:SEQ, :MLA_ROPE] = np.cos(ang)
    sign = np.where((np.arange(MLA_ROPE) // n_freq) % 2 == 0, -1.0, 1.0).astype(np.float32)
    sin[:SEQ, :MLA_ROPE] = np.sin(ang) * sign[None, :]
    return jnp.asarray(cos), jnp.asarray(sin)


def _permute_w_in(w):
    seg = lambda a, n: w[:, a:a + n]
    parts = [
        seg(0, 4096),
        seg(5120, 1024), seg(6144, 1024),
        seg(7776, 1024),
        seg(8800, 3072),
        seg(4096, 512), seg(4608, 512),
        seg(7200, 384), seg(7584, 128),
        seg(7712, 64), seg(7168, 16), seg(7184, 16),
    ]
    used = sum(p.shape[1] for p in parts)
    parts.append(jnp.zeros((D, IN_PAD - used), w.dtype))
    return jnp.concatenate(parts, axis=1).astype(BF16)


def kernel(x, c, ctx, c_ctx, w_mod, b_mod, norm_g, w_in, conv_w, gla_wa_up_f, gla_ba_f, gla_wa_up_b,
           gla_ba_b, gla_norm_g, mla_q_norm_g, mla_kv_norm_g, mla_wq_up, mla_wkv_up, mla_qn_g, mla_kn_g,
           w_br_a, w_br_b, w_br_c, w_out):
    cos_t, sin_t = _rope_tables()
    xs = jnp.concatenate([x, ctx], axis=1).reshape(N_ROWS, D)
    cc = jnp.concatenate([c, c_ctx[None, :], jnp.zeros((8 - BATCH - 1, D), F32)], axis=0)
    mods = _modulation(cc, w_mod, b_mod).reshape(DEPTH, 8, 1, 3 * D)

    hk = GLA_H * GLA_DK
    for l in range(DEPTH):
        last = l == DEPTH - 1
        w_in_p = _permute_w_in(w_in[l])
        wup_p = jnp.zeros((128, 2 * hk), F32)
        wup_p = wup_p.at[SMALL_AF:SMALL_AF + GLA_RANK, :hk].set(gla_wa_up_f[l])
        wup_p = wup_p.at[SMALL_AB:SMALL_AB + GLA_RANK, hk:].set(gla_wa_up_b[l]).astype(BF16)
        bup_p = jnp.concatenate([gla_ba_f[l], gla_ba_b[l]])[None, :]
        wq_p = jnp.pad(mla_wq_up[l].reshape(MLA_Q_RANK, MLA_H, MLA_QK),
                       ((0, 0), (0, 0), (0, MLA_QPAD - MLA_QK))).reshape(MLA_Q_RANK, MLA_H * MLA_QPAD).astype(BF16)
        qn_p = jnp.pad(mla_qn_g[l], (0, MLA_QPAD - MLA_QK))[None, :]
        kn_p = jnp.pad(mla_kn_g[l], (0, MLA_QPAD - MLA_QK))[None, :]

        u = _inproj(xs, mods[l], norm_g[l][None, :], w_in_p)
        q, k, v = _mla_prep(u, cos_t, sin_t, wq_p, mla_wkv_up[l].astype(BF16),
                            mla_q_norm_g[l][None, :], mla_kv_norm_g[l][None, :], qn_p, kn_p)
        o_c = _attention(q, k, v)
        o_f, o_b = _gla(u, wup_p, bup_p)
        xs = _merge(u, o_f, o_b, o_c, xs, mods[l], conv_w[l], gla_norm_g[l][None, :],
                    w_br_a[l].astype(BF16), w_br_b[l].astype(BF16), w_br_c[l].astype(BF16),
                    w_out[l].astype(BF16), last)
    return xs.reshape(BATCH, SEQ, D)
```

```python
import functools
import math

import numpy as np
import jax
import jax.numpy as jnp
from jax import lax
from jax.experimental import pallas as pl
from jax.experimental.pallas import tpu as pltpu

D = 1024
BATCH = 4
SEQ = 4096
DEPTH = 4
GRID_W = 64
CTX = 256
EPS = 1e-6

GLA_H = 4
GLA_DK = 128
GLA_DV = 256
GLA_RANK = 16
GLA_TAU = 16.0
CHUNK = 64

MLA_H = 8
MLA_NOPE = 128
MLA_ROPE = 64
MLA_DV = 128
MLA_QK = MLA_NOPE + MLA_ROPE
MLA_QPAD = 256
VT_ROWS = MLA_DV + 16
_SM_SCALE_LOG2E = (MLA_QK ** -0.5) * math.log2(math.e)
MLA_Q_RANK = 384
MLA_KV_RANK = 128
ROPE_BASE = 10000.0

SB = SEQ + CTX
N_ROWS = BATCH * SB
ROW_TILE = 256
TILES_PER_BATCH = SB // ROW_TILE
X_TILES = SEQ // ROW_TILE
N_TILES = N_ROWS // ROW_TILE
CTX_GROUP = BATCH

IN_PAD = 12288
COL_AV, COL_AB, COL_AC, COL_AZ, COL_BV, COL_BZ, COL_CZ, COL_GA, COL_GB, COL_GC = range(10)
COL_BQ, COL_BK, COL_CQKV = 20, 21, 22
COL_SMALL = 92
SMALL_KR, SMALL_AF, SMALL_AB = 0, 64, 80

IN_TN = 2048
IN_TILES_PER_BATCH = 4
IN_TM = SB // IN_TILES_PER_BATCH
Q_TILE = 1024
KV_TILE = 1024

VMEM_LIMIT = 56 * 1024 * 1024

F32 = jnp.float32
BF16 = jnp.bfloat16


def _dot(a, b):
    return jnp.dot(a, b, preferred_element_type=F32)


def _dot_nt(a, b):
    return lax.dot_general(a, b, (((1,), (1,)), ((), ())), preferred_element_type=F32)


def _dot_exact(a, b):
    return jnp.dot(a, b, preferred_element_type=F32, precision=lax.Precision.HIGHEST)


def _sigmoid(x):
    return 1.0 / (1.0 + jnp.exp(-x))


def _silu(x):
    return x * _sigmoid(x)


def _tile_group(i):
    tt = i % TILES_PER_BATCH
    return jnp.where(tt >= X_TILES, CTX_GROUP, i // TILES_PER_BATCH)


def _mod_kernel(cc_ref, w_ref, b_ref, o_ref):
    o_ref[0] = _dot_exact(_silu(cc_ref[...]), w_ref[0]) + b_ref[0]


def _modulation(cc, w_mod, b_mod):
    return pl.pallas_call(
        _mod_kernel,
        grid=(DEPTH,),
        in_specs=[
            pl.BlockSpec((8, D), lambda l: (0, 0)),
            pl.BlockSpec((1, D, 3 * D), lambda l: (l, 0, 0)),
            pl.BlockSpec((1, 1, 3 * D), lambda l: (l, 0, 0)),
        ],
        out_specs=pl.BlockSpec((1, 8, 3 * D), lambda l: (l, 0, 0)),
        out_shape=jax.ShapeDtypeStruct((DEPTH, 8, 3 * D), F32),
        compiler_params=pltpu.CompilerParams(
            dimension_semantics=("arbitrary",), vmem_limit_bytes=VMEM_LIMIT),
        name="modulation",
    )(cc, w_mod, b_mod.reshape(DEPTH, 1, 3 * D))


def _inproj_kernel(x_ref, modx_ref, modc_ref, g_ref, w_ref, u_ref, h_ref):
    @pl.when(pl.program_id(1) == 0)
    def _():
        x = x_ref[...]
        y = x * lax.rsqrt(jnp.mean(x * x, axis=-1, keepdims=True) + EPS) * g_ref[...]
        row = (pl.program_id(0) % IN_TILES_PER_BATCH) * IN_TM + lax.broadcasted_iota(jnp.int32, x.shape, 0)
        is_ctx = row >= SEQ
        modx = modx_ref[0]
        modc = modc_ref[0]
        scale = jnp.where(is_ctx, modc[:, D:2 * D], modx[:, D:2 * D])
        shift = jnp.where(is_ctx, modc[:, 0:D], modx[:, 0:D])
        h_ref[...] = (y * (1.0 + scale) + shift).astype(BF16)

    u_ref[...] = _dot(h_ref[...], w_ref[...])


def _inproj(xs, mod_l, norm_g, w_in_p):
    return pl.pallas_call(
        _inproj_kernel,
        grid=(N_ROWS // IN_TM, IN_PAD // IN_TN),
        in_specs=[
            pl.BlockSpec((IN_TM, D), lambda i, j: (i, 0)),
            pl.BlockSpec((1, 1, 3 * D), lambda i, j: (i // IN_TILES_PER_BATCH, 0, 0)),
            pl.BlockSpec((1, 1, 3 * D), lambda i, j: (CTX_GROUP, 0, 0)),
            pl.BlockSpec((1, D), lambda i, j: (0, 0)),
            pl.BlockSpec((D, IN_TN), lambda i, j: (0, j)),
        ],
        out_specs=pl.BlockSpec((IN_TM, IN_TN), lambda i, j: (i, j)),
        out_shape=jax.ShapeDtypeStruct((N_ROWS, IN_PAD), F32),
        scratch_shapes=[pltpu.VMEM((IN_TM, D), BF16)],
        compiler_params=pltpu.CompilerParams(
            dimension_semantics=("arbitrary", "arbitrary"), vmem_limit_bytes=VMEM_LIMIT),
        name="inproj",
    )(xs, mod_l, mod_l, norm_g, w_in_p)


def _swap16(r):
    lane = lax.broadcasted_iota(jnp.int32, r.shape, 1)
    even = ((lane // 16) % 2) == 0
    return jnp.where(even, pltpu.roll(r, 112, axis=1), pltpu.roll(r, 16, axis=1))


def _mla_prep_kernel(cqkv_ref, small_ref, cos_ref, sin_ref, qtab_ref, wq_ref, wk_ref, wvt_ref, gq_ref, gkv_ref,
                     qn_ref, kn_ref, q_ref, k_ref, vt_ref):
    cqkv = cqkv_ref[...]
    cq = cqkv[:, :MLA_Q_RANK]
    ckv = cqkv[:, MLA_Q_RANK:]

    cq_n = cq * lax.rsqrt(jnp.mean(cq * cq, axis=-1, keepdims=True) + EPS) * gq_ref[...]
    q_all = _dot(cq_n.astype(BF16), wq_ref[...])
    ckv_n = (ckv * lax.rsqrt(jnp.mean(ckv * ckv, axis=-1, keepdims=True) + EPS) * gkv_ref[...]).astype(BF16)
    kn_all = _dot(ckv_n, wk_ref[...])
    vt_all = _dot_nt(wvt_ref[...], ckv_n)

    small = small_ref[...]
    lane = lax.broadcasted_iota(jnp.int32, small.shape, 1)
    kr = jnp.where(lane < MLA_ROPE, small, 0.0)
    kr_ss = jnp.sum(kr * kr, axis=-1, keepdims=True)
    qn_g = qn_ref[...] * _SM_SCALE_LOG2E
    kn_g = kn_ref[...]
    krg = kr * kn_g[:, MLA_NOPE:MLA_QPAD]
    k_rot = krg * cos_ref[...] + _swap16(krg) * sin_ref[...]
    k_rot2 = k_rot + pltpu.roll(k_rot, MLA_ROPE, axis=1)
    qtab = qtab_ref[...]
    ones_rows = jnp.ones((VT_ROWS - MLA_DV, ROW_TILE), BF16)

    for h in range(MLA_H):
        q0 = q_all[:, h * MLA_QPAD:h * MLA_QPAD + MLA_NOPE]
        q1 = q_all[:, h * MLA_QPAD + MLA_NOPE:(h + 1) * MLA_QPAD]
        ss = jnp.sum(q0 * q0 + 0.5 * (q1 * q1), axis=-1, keepdims=True)
        inv = lax.rsqrt(ss * (1.0 / MLA_QK) + EPS)
        q_ref[h, :, 0:MLA_NOPE] = (q0 * inv * qn_g[:, 0:MLA_NOPE]).astype(BF16)
        q_ref[h, :, MLA_NOPE:MLA_QPAD] = (q1 * inv * qn_g[:, MLA_NOPE:MLA_QPAD] * qtab).astype(BF16)

        kn = kn_all[:, h * MLA_NOPE:(h + 1) * MLA_NOPE]
        inv = lax.rsqrt((jnp.sum(kn * kn, axis=-1, keepdims=True) + kr_ss) * (1.0 / MLA_QK) + EPS)
        k_ref[h, :, 0:MLA_NOPE] = (kn * inv * kn_g[:, 0:MLA_NOPE]).astype(BF16)
        k_ref[h, :, MLA_NOPE:MLA_QPAD] = (k_rot2 * inv).astype(BF16)
        vt_ref[h, 0:MLA_DV, :] = vt_all[h * MLA_DV:(h + 1) * MLA_DV, :].astype(BF16)
        vt_ref[h, MLA_DV:VT_ROWS, :] = ones_rows


def _mla_prep(u, cos_t, sin_t, qtab_t, wq_p, wk, wvt, gq, gkv, qn_p, kn_p):
    const = lambda i: (0, 0)
    table = pl.BlockSpec((ROW_TILE, 128), lambda i: (i % TILES_PER_BATCH, 0))
    return pl.pallas_call(
        _mla_prep_kernel,
        grid=(N_TILES,),
        in_specs=[
            pl.BlockSpec((ROW_TILE, 512), lambda i: (i, COL_CQKV)),
            pl.BlockSpec((ROW_TILE, 128), lambda i: (i, COL_SMALL)),
            table, table, table,
            pl.BlockSpec((MLA_Q_RANK, MLA_H * MLA_QPAD), const),
            pl.BlockSpec((MLA_KV_RANK, MLA_H * MLA_NOPE), const),
            pl.BlockSpec((MLA_H * MLA_DV, MLA_KV_RANK), const),
            pl.BlockSpec((1, MLA_Q_RANK), const),
            pl.BlockSpec((1, MLA_KV_RANK), const),
            pl.BlockSpec((1, MLA_QPAD), const),
            pl.BlockSpec((1, MLA_QPAD), const),
        ],
        out_specs=[
            pl.BlockSpec((MLA_H, ROW_TILE, MLA_QPAD), lambda i: (0, i, 0)),
            pl.BlockSpec((MLA_H, ROW_TILE, MLA_QPAD), lambda i: (0, i, 0)),
            pl.BlockSpec((MLA_H, VT_ROWS, ROW_TILE), lambda i: (0, 0, i)),
        ],
        out_shape=[
            jax.ShapeDtypeStruct((MLA_H, N_ROWS, MLA_QPAD), BF16),
            jax.ShapeDtypeStruct((MLA_H, N_ROWS, MLA_QPAD), BF16),
            jax.ShapeDtypeStruct((MLA_H, VT_ROWS, N_ROWS), BF16),
        ],
        compiler_params=pltpu.CompilerParams(
            dimension_semantics=("arbitrary",), vmem_limit_bytes=VMEM_LIMIT),
        name="mla_prep",
    )(u, u, cos_t, sin_t, qtab_t, wq_p, wk, wvt, gq, gkv, qn_p, kn_p)


def _attn_kernel(q_ref, k_ref, vt_ref, o_ref):
    def softmax_tile(q, kv_bounds):
        m = acc = None
        scores = lambda b: _dot_nt(k_ref[0, b[0]:b[1], :], q)
        s_next = scores(kv_bounds[0])
        for c, (lo, hi) in enumerate(kv_bounds):
            s = s_next
            if c + 1 < len(kv_bounds):
                s_next = scores(kv_bounds[c + 1])
            m_c = jnp.max(s, axis=0, keepdims=True)
            m_new = m_c if m is None else jnp.maximum(m, m_c)
            p = jnp.exp2(s - m_new)
            pv = _dot(vt_ref[0, :, lo:hi], p.astype(BF16))
            acc = pv if m is None else jnp.exp2(m - m_new) * acc + pv
            m = m_new
        return (acc[0:MLA_DV] / acc[MLA_DV:MLA_DV + 1]).T

    lat_bounds = [(c * KV_TILE, (c + 1) * KV_TILE) for c in range(SEQ // KV_TILE)] + [(SEQ, SB)]

    def q_step(i, carry):
        r0 = pl.multiple_of(i * Q_TILE, Q_TILE)
        o_ref[pl.ds(r0, Q_TILE), :] = softmax_tile(q_ref[0, pl.ds(r0, Q_TILE), :], lat_bounds)
        return carry

    lax.fori_loop(0, SEQ // Q_TILE, q_step, 0)
    o_ref[SEQ:SB, :] = softmax_tile(q_ref[0, SEQ:SB, :], [(SEQ, SB)])


def _attention(q, k, vt):
    return pl.pallas_call(
        _attn_kernel,
        grid=(BATCH, MLA_H),
        in_specs=[
            pl.BlockSpec((1, SB, MLA_QPAD), lambda b, h: (h, b, 0)),
            pl.BlockSpec((1, SB, MLA_QPAD), lambda b, h: (h, b, 0)),
            pl.BlockSpec((1, VT_ROWS, SB), lambda b, h: (h, 0, b)),
        ],
        out_specs=pl.BlockSpec((SB, MLA_DV), lambda b, h: (b, h)),
        out_shape=jax.ShapeDtypeStruct((N_ROWS, MLA_H * MLA_DV), F32),
        compiler_params=pltpu.CompilerParams(
            dimension_semantics=("arbitrary", "arbitrary"), vmem_limit_bytes=VMEM_LIMIT),
        name="mla_attention",
    )(q, k, vt)


def _log_sigmoid(z):
    return jnp.minimum(z, 0.0) - jnp.log1p(jnp.exp(-jnp.abs(z)))


def _gla_kernel(qf_ref, kf_ref, vf_ref, sf_ref, qb_ref, kb_ref, vb_ref, sb_ref, wup_ref, bup_ref,
                of_ref, ob_ref, st_ref):
    @pl.when(pl.program_id(1) == 0)
    def _():
        st_ref[...] = jnp.zeros_like(st_ref)

    hk = GLA_H * GLA_DK
    row = lax.broadcasted_iota(jnp.int32, (CHUNK, CHUNK), 0)
    col = lax.broadcasted_iota(jnp.int32, (CHUNK, CHUNK), 1)
    q_scale = GLA_DK ** -0.5

    def direction(d, q_ref, k_ref, v_ref, s_ref, o_ref):
        keep = (row >= col) if d == 0 else (row <= col)
        tri = keep.astype(F32)
        z = _dot(s_ref[...].astype(BF16), wup_ref[:, d * hk:(d + 1) * hk]) + bup_ref[:, d * hk:(d + 1) * hk]
        la = _log_sigmoid(z) * (1.0 / GLA_TAU)
        n_chunks = ROW_TILE // CHUNK
        order = range(n_chunks) if d == 0 else range(n_chunks - 1, -1, -1)
        for c in order:
            r0, r1 = c * CHUNK, (c + 1) * CHUNK
            b = _dot_exact(tri, la[r0:r1])
            tot = b[CHUNK - 1:CHUNK] if d == 0 else b[0:1]
            q_dec = (q_ref[r0:r1, :] * q_scale) * jnp.exp(b)
            kc = k_ref[r0:r1, :]
            k_inv = (kc * jnp.exp(-b)).astype(BF16)
            k_end = (kc * jnp.exp(tot - b)).astype(BF16)
            g = jnp.exp(tot)
            q_dec = q_dec.astype(BF16)
            for h in range(GLA_H):
                ks = slice(h * GLA_DK, (h + 1) * GLA_DK)
                vs = slice(h * GLA_DV, (h + 1) * GLA_DV)
                vh = v_ref[r0:r1, vs]
                vh_b = vh.astype(BF16)
                att = jnp.where(keep, _dot_nt(q_dec[:, ks], k_inv[:, ks]), 0.0)
                st = st_ref[d, h]
                o = _dot(att.astype(BF16), vh_b) + _dot_nt(q_dec[:, ks], st.astype(BF16))
                o_ref[r0:r1, vs] = o
                upd = _dot(vh.T.astype(BF16), k_end[:, ks])
                st_ref[d, h] = st * g[:, ks] + upd

    direction(0, qf_ref, kf_ref, vf_ref, sf_ref, of_ref)
    direction(1, qb_ref, kb_ref, vb_ref, sb_ref, ob_ref)


def _gla(u, wup_p, bup_p):
    def fwd_tile(b, s):
        return b * TILES_PER_BATCH + jnp.where(s == 0, X_TILES, s - 1)

    def bwd_tile(b, s):
        return b * TILES_PER_BATCH + jnp.where(s == 0, X_TILES, X_TILES - s)

    def specs(tile):
        return [
            pl.BlockSpec((ROW_TILE, 512), lambda b, s: (tile(b, s), COL_BQ)),
            pl.BlockSpec((ROW_TILE, 512), lambda b, s: (tile(b, s), COL_BK)),
            pl.BlockSpec((ROW_TILE, 1024), lambda b, s: (tile(b, s), COL_BV)),
            pl.BlockSpec((ROW_TILE, 128), lambda b, s: (tile(b, s), COL_SMALL)),
        ]

    return pl.pallas_call(
        _gla_kernel,
        grid=(BATCH, TILES_PER_BATCH),
        in_specs=specs(fwd_tile) + specs(bwd_tile) + [
            pl.BlockSpec((128, 2 * GLA_H * GLA_DK), lambda b, s: (0, 0)),
            pl.BlockSpec((1, 2 * GLA_H * GLA_DK), lambda b, s: (0, 0)),
        ],
        out_specs=[
            pl.BlockSpec((ROW_TILE, GLA_H * GLA_DV), lambda b, s: (fwd_tile(b, s), 0)),
            pl.BlockSpec((ROW_TILE, GLA_H * GLA_DV), lambda b, s: (bwd_tile(b, s), 0)),
        ],
        out_shape=[jax.ShapeDtypeStruct((N_ROWS, GLA_H * GLA_DV), F32)] * 2,
        scratch_shapes=[pltpu.VMEM((2, GLA_H, GLA_DV, GLA_DK), F32)],
        compiler_params=pltpu.CompilerParams(
            dimension_semantics=("arbitrary", "arbitrary"), vmem_limit_bytes=VMEM_LIMIT),
        name="gla_bidir",
    )(u, u, u, u, u, u, u, u, wup_p, bup_p)


def _merge_kernel(av_ref, ab_ref, ac_ref, az_ref, bz_ref, cz_ref, ga_ref, gb_ref, gc_ref,
                  avp_ref, acp_ref, avn_ref, acn_ref, of_ref, ob_ref, oc_ref, x_ref, mod_ref,
                  cw_ref, gng_ref, wa_ref, wb_ref, wc_ref, wo_ref, out_ref, *, tiles_per_batch):
    tt = pl.program_id(0) % tiles_per_batch
    prev_ok = jnp.logical_and(tt != 0, tt < X_TILES).astype(F32)
    next_ok = (tt < X_TILES - 1).astype(F32)

    p = ac_ref[...] * av_ref[...]
    p_prev = acp_ref[7:8, :] * avp_ref[7:8, :] * prev_ok
    p_next = acn_ref[0:1, :] * avn_ref[0:1, :] * next_ok
    rows = lax.broadcasted_iota(jnp.int32, p.shape, 0)
    p_up = jnp.where(rows == 0, p_prev, pltpu.roll(p, 1, axis=0))
    p_dn = jnp.where(rows == ROW_TILE - 1, p_next, pltpu.roll(p, ROW_TILE - 1, axis=0))
    cw = cw_ref[...]
    conv = p_up * cw[0:1] + p * cw[1:2] + p_dn * cw[2:3]
    y_a = ab_ref[...] * conv * _silu(az_ref[...])
    m = _sigmoid(ga_ref[...]) * _dot(y_a.astype(BF16), wa_ref[...])

    o = of_ref[...] + ob_ref[...]
    gng = gng_ref[...]
    y_b = []
    for h in range(GLA_H):
        sl = slice(h * GLA_DV, (h + 1) * GLA_DV)
        oh = o[:, sl]
        y_b.append(oh * lax.rsqrt(jnp.mean(oh * oh, axis=-1, keepdims=True) + EPS) * gng[:, sl])
    y_b = jnp.concatenate(y_b, axis=-1) * _silu(bz_ref[...])
    m = m + _sigmoid(gb_ref[...]) * _dot(y_b.astype(BF16), wb_ref[...])

    y_c = oc_ref[...] * _silu(cz_ref[...])
    m = m + _sigmoid(gc_ref[...]) * _dot(y_c.astype(BF16), wc_ref[...])

    gate = mod_ref[0][:, 2 * D:3 * D]
    out_ref[...] = x_ref[...] + gate * _dot(m.astype(BF16), wo_ref[...])


def _merge(u, o_f, o_b, o_c, xs, mod_l, conv_w, gla_norm_g, wa, wb, wc, wo, last):
    halo_blocks = ROW_TILE // 8
    n_halo = N_ROWS // 8
    if last:
        grid = (BATCH * X_TILES,)
        tile = lambda i: (i // X_TILES) * TILES_PER_BATCH + i % X_TILES
        out_rows = BATCH * SEQ
        out_map = lambda i: (i, 0)
    else:
        grid = (N_TILES,)
        tile = lambda i: i
        out_rows = N_ROWS
        out_map = lambda i: (i, 0)

    def col(c):
        return pl.BlockSpec((ROW_TILE, D), lambda i: (tile(i), c))

    def prev_rows(c):
        return pl.BlockSpec((8, D), lambda i: (jnp.maximum(tile(i) * halo_blocks - 1, 0), c))

    def next_rows(c):
        return pl.BlockSpec((8, D), lambda i: (jnp.minimum((tile(i) + 1) * halo_blocks, n_halo - 1), c))

    row_block = pl.BlockSpec((ROW_TILE, D), lambda i: (tile(i), 0))
    const = lambda i: (0, 0)
    weight = pl.BlockSpec((D, D), const)

    return pl.pallas_call(
        functools.partial(_merge_kernel, tiles_per_batch=X_TILES if last else TILES_PER_BATCH),
        grid=grid,
        in_specs=[col(COL_AV), col(COL_AB), col(COL_AC), col(COL_AZ), col(COL_BZ), col(COL_CZ),
                  col(COL_GA), col(COL_GB), col(COL_GC),
                  prev_rows(COL_AV), prev_rows(COL_AC), next_rows(COL_AV), next_rows(COL_AC),
                  row_block, row_block, row_block, row_block,
                  pl.BlockSpec((1, 1, 3 * D), lambda i: (_tile_group(tile(i)), 0, 0)),
                  pl.BlockSpec((3, D), const), pl.BlockSpec((1, D), const),
                  weight, weight, weight, weight],
        out_specs=pl.BlockSpec((ROW_TILE, D), out_map),
        out_shape=jax.ShapeDtypeStruct((out_rows, D), F32),
        compiler_params=pltpu.CompilerParams(
            dimension_semantics=("arbitrary",), vmem_limit_bytes=VMEM_LIMIT),
        name="merge_last" if last else "merge",
    )(u, u, u, u, u, u, u, u, u, u, u, u, u, o_f, o_b, o_c, xs, mod_l, conv_w, gla_norm_g,
      wa, wb, wc, wo)


def _rope_tables():
    t = np.arange(SEQ)
    row = (t // GRID_W).astype(np.float32)
    colp = (t % GRID_W).astype(np.float32)
    n_freq = MLA_ROPE // 4
    freqs = (np.float32(ROPE_BASE) ** (-np.arange(n_freq, dtype=np.float32) / np.float32(n_freq))).astype(np.float32)
    ang_r = row[:, None] * freqs[None, :]
    ang_c = colp[:, None] * freqs[None, :]
    ang = np.concatenate([ang_r, ang_r, ang_c, ang_c], axis=-1).astype(np.float32)
    cos = np.ones((SB, 128), np.float32)
    sin = np.zeros((SB, 128), np.float32)
    cos[:SEQ, :MLA_ROPE] = np.cos(ang)
    sign = np.where((np.arange(MLA_ROPE) // n_freq) % 2 == 0, -1.0, 1.0).astype(np.float32)
    sin[:SEQ, :MLA_ROPE] = np.sin(ang) * sign[None, :]
    qtab = np.concatenate([cos[:, :MLA_ROPE], sin[:, :MLA_ROPE]], axis=1)
    return jnp.asarray(cos), jnp.asarray(sin), jnp.asarray(qtab)


def _swap16_index():
    l = np.arange(MLA_ROPE)
    return np.where((l // 16) % 2 == 0, l + 16, l - 16)


def _permute_w_in(w):
    seg = lambda a, n: w[:, a:a + n]
    parts = [
        seg(0, 4096),
        seg(5120, 1024), seg(6144, 1024),
        seg(7776, 1024),
        seg(8800, 3072),
        seg(4096, 512), seg(4608, 512),
        seg(7200, 384), seg(7584, 128),
        seg(7712, 64), seg(7168, 16), seg(7184, 16),
    ]
    used = sum(p.shape[1] for p in parts)
    parts.append(jnp.zeros((D, IN_PAD - used), w.dtype))
    return jnp.concatenate(parts, axis=1).astype(BF16)


def kernel(x, c, ctx, c_ctx, w_mod, b_mod, norm_g, w_in, conv_w, gla_wa_up_f, gla_ba_f, gla_wa_up_b,
           gla_ba_b, gla_norm_g, mla_q_norm_g, mla_kv_norm_g, mla_wq_up, mla_wkv_up, mla_qn_g, mla_kn_g,
           w_br_a, w_br_b, w_br_c, w_out):
    cos_t, sin_t, qtab_t = _rope_tables()
    perm = _swap16_index()
    xs = jnp.concatenate([x, ctx], axis=1).reshape(N_ROWS, D)
    cc = jnp.concatenate([c, c_ctx[None, :], jnp.zeros((8 - BATCH - 1, D), F32)], axis=0)
    mods = _modulation(cc, w_mod, b_mod).reshape(DEPTH, 8, 1, 3 * D)

    hk = GLA_H * GLA_DK
    for l in range(DEPTH):
        last = l == DEPTH - 1
        w_in_p = _permute_w_in(w_in[l])
        wup_p = jnp.zeros((128, 2 * hk), F32)
        wup_p = wup_p.at[SMALL_AF:SMALL_AF + GLA_RANK, :hk].set(gla_wa_up_f[l])
        wup_p = wup_p.at[SMALL_AB:SMALL_AB + GLA_RANK, hk:].set(gla_wa_up_b[l]).astype(BF16)
        bup_p = jnp.concatenate([gla_ba_f[l], gla_ba_b[l]])[None, :]
        wq3 = mla_wq_up[l].reshape(MLA_Q_RANK, MLA_H, MLA_QK)
        wq_rope = wq3[:, :, MLA_NOPE:]
        wq_p = jnp.concatenate([wq3, wq_rope[:, :, perm]], axis=2).reshape(MLA_Q_RANK, MLA_H * MLA_QPAD).astype(BF16)
        qn_rope = mla_qn_g[l][MLA_NOPE:]
        qn_p = jnp.concatenate([mla_qn_g[l], qn_rope[perm]])[None, :]
        kn_p = jnp.pad(mla_kn_g[l], (0, MLA_QPAD - MLA_QK))[None, :]
        wkv3 = mla_wkv_up[l].reshape(MLA_KV_RANK, MLA_H, MLA_NOPE + MLA_DV)
        wk = wkv3[:, :, :MLA_NOPE].reshape(MLA_KV_RANK, MLA_H * MLA_NOPE).astype(BF16)
        wvt = wkv3[:, :, MLA_NOPE:].reshape(MLA_KV_RANK, MLA_H * MLA_DV).T.astype(BF16)

        u = _inproj(xs, mods[l], norm_g[l][None, :], w_in_p)
        q, k, vt = _mla_prep(u, cos_t, sin_t, qtab_t, wq_p, wk, wvt,
                             mla_q_norm_g[l][None, :], mla_kv_norm_g[l][None, :], qn_p, kn_p)
        o_c = _attention(q, k, vt)
        o_f, o_b = _gla(u, wup_p, bup_p)
        xs = _merge(u, o_f, o_b, o_c, xs, mods[l], conv_w[l], gla_norm_g[l][None, :],
                    w_br_a[l].astype(BF16), w_br_b[l].astype(BF16), w_br_c[l].astype(BF16),
                    w_out[l].astype(BF16), last)
    return xs.reshape(BATCH, SEQ, D)
```

```python
import functools
import math

import numpy as np
import jax
import jax.numpy as jnp
from jax import lax
from jax.experimental import pallas as pl
from jax.experimental.pallas import tpu as pltpu

D = 1024
BATCH = 4
SEQ = 4096
DEPTH = 4
GRID_W = 64
CTX = 256
EPS = 1e-6

GLA_H = 4
GLA_DK = 128
GLA_DV = 256
GLA_RANK = 16
GLA_TAU = 16.0
CHUNK = 64

MLA_H = 8
MLA_NOPE = 128
MLA_ROPE = 64
MLA_DV = 128
MLA_QK = MLA_NOPE + MLA_ROPE
MLA_QPAD = 256
VT_ROWS = MLA_DV + 16
_SM_SCALE_LOG2E = (MLA_QK ** -0.5) * math.log2(math.e)
MLA_Q_RANK = 384
MLA_KV_RANK = 128
ROPE_BASE = 10000.0

SB = SEQ + CTX
N_ROWS = BATCH * SB
ROW_TILE = 256
TILES_PER_BATCH = SB // ROW_TILE
X_TILES = SEQ // ROW_TILE
N_TILES = N_ROWS // ROW_TILE
CTX_GROUP = BATCH

IN_PAD = 12288
COL_AV, COL_AB, COL_AC, COL_AZ, COL_BV, COL_BZ, COL_CZ, COL_GA, COL_GB, COL_GC = range(10)
COL_BQ, COL_BK, COL_CQKV = 20, 21, 22
COL_SMALL = 92
SMALL_KR, SMALL_AF, SMALL_AB = 0, 64, 80

IN_TN = 2048
IN_TILES_PER_BATCH = 4
IN_TM = SB // IN_TILES_PER_BATCH
Q_TILE = 1024
KV_TILE = 1024

VMEM_LIMIT = 56 * 1024 * 1024

F32 = jnp.float32
BF16 = jnp.bfloat16
ACT = BF16
HALO = 16


def _dot(a, b):
    return jnp.dot(a, b, preferred_element_type=F32)


def _dot_nt(a, b):
    return lax.dot_general(a, b, (((1,), (1,)), ((), ())), preferred_element_type=F32)


def _dot_exact(a, b):
    return jnp.dot(a, b, preferred_element_type=F32, precision=lax.Precision.HIGHEST)


def _sigmoid(x):
    return 1.0 / (1.0 + jnp.exp(-x))


def _silu(x):
    return x * _sigmoid(x)


def _tile_group(i):
    tt = i % TILES_PER_BATCH
    return jnp.where(tt >= X_TILES, CTX_GROUP, i // TILES_PER_BATCH)


def _mod_kernel(cc_ref, w_ref, b_ref, o_ref):
    o_ref[0] = _dot_exact(_silu(cc_ref[...]), w_ref[0]) + b_ref[0]


def _modulation(cc, w_mod, b_mod):
    return pl.pallas_call(
        _mod_kernel,
        grid=(DEPTH,),
        in_specs=[
            pl.BlockSpec((8, D), lambda l: (0, 0)),
            pl.BlockSpec((1, D, 3 * D), lambda l: (l, 0, 0)),
            pl.BlockSpec((1, 1, 3 * D), lambda l: (l, 0, 0)),
        ],
        out_specs=pl.BlockSpec((1, 8, 3 * D), lambda l: (l, 0, 0)),
        out_shape=jax.ShapeDtypeStruct((DEPTH, 8, 3 * D), F32),
        compiler_params=pltpu.CompilerParams(
            dimension_semantics=("arbitrary",), vmem_limit_bytes=VMEM_LIMIT),
        name="modulation",
    )(cc, w_mod, b_mod.reshape(DEPTH, 1, 3 * D))


def _inproj_kernel(x_ref, modx_ref, modc_ref, g_ref, w_ref, u_ref, h_ref):
    @pl.when(pl.program_id(1) == 0)
    def _():
        x = x_ref[...]
        y = x * lax.rsqrt(jnp.mean(x * x, axis=-1, keepdims=True) + EPS) * g_ref[...]
        row = (pl.program_id(0) % IN_TILES_PER_BATCH) * IN_TM + lax.broadcasted_iota(jnp.int32, x.shape, 0)
        is_ctx = row >= SEQ
        modx = modx_ref[0]
        modc = modc_ref[0]
        scale = jnp.where(is_ctx, modc[:, D:2 * D], modx[:, D:2 * D])
        shift = jnp.where(is_ctx, modc[:, 0:D], modx[:, 0:D])
        h_ref[...] = (y * (1.0 + scale) + shift).astype(BF16)

    u_ref[...] = _dot(h_ref[...], w_ref[...]).astype(u_ref.dtype)


def _inproj(xs, mod_l, norm_g, w_in_p):
    return pl.pallas_call(
        _inproj_kernel,
        grid=(N_ROWS // IN_TM, IN_PAD // IN_TN),
        in_specs=[
            pl.BlockSpec((IN_TM, D), lambda i, j: (i, 0)),
            pl.BlockSpec((1, 1, 3 * D), lambda i, j: (i // IN_TILES_PER_BATCH, 0, 0)),
            pl.BlockSpec((1, 1, 3 * D), lambda i, j: (CTX_GROUP, 0, 0)),
            pl.BlockSpec((1, D), lambda i, j: (0, 0)),
            pl.BlockSpec((D, IN_TN), lambda i, j: (0, j)),
        ],
        out_specs=pl.BlockSpec((IN_TM, IN_TN), lambda i, j: (i, j)),
        out_shape=jax.ShapeDtypeStruct((N_ROWS, IN_PAD), ACT),
        scratch_shapes=[pltpu.VMEM((IN_TM, D), BF16)],
        compiler_params=pltpu.CompilerParams(
            dimension_semantics=("arbitrary", "arbitrary"), vmem_limit_bytes=VMEM_LIMIT),
        name="inproj",
    )(xs, mod_l, mod_l, norm_g, w_in_p)


def _swap16(r):
    lane = lax.broadcasted_iota(jnp.int32, r.shape, 1)
    even = ((lane // 16) % 2) == 0
    return jnp.where(even, pltpu.roll(r, 112, axis=1), pltpu.roll(r, 16, axis=1))


def _mla_prep_kernel(cqkv_ref, small_ref, cos_ref, sin_ref, qtab_ref, wq_ref, wk_ref, wvt_ref, gq_ref, gkv_ref,
                     qn_ref, kn_ref, q_ref, k_ref, vt_ref):
    cqkv = cqkv_ref[...].astype(F32)
    cq = cqkv[:, :MLA_Q_RANK]
    ckv = cqkv[:, MLA_Q_RANK:]

    cq_n = cq * lax.rsqrt(jnp.mean(cq * cq, axis=-1, keepdims=True) + EPS) * gq_ref[...]
    q_all = _dot(cq_n.astype(BF16), wq_ref[...])
    ckv_n = (ckv * lax.rsqrt(jnp.mean(ckv * ckv, axis=-1, keepdims=True) + EPS) * gkv_ref[...]).astype(BF16)
    kn_all = _dot(ckv_n, wk_ref[...])
    vt_all = _dot_nt(wvt_ref[...], ckv_n)

    small = small_ref[...].astype(F32)
    lane = lax.broadcasted_iota(jnp.int32, small.shape, 1)
    kr = jnp.where(lane < MLA_ROPE, small, 0.0)
    kr_ss = jnp.sum(kr * kr, axis=-1, keepdims=True)
    qn_g = qn_ref[...] * _SM_SCALE_LOG2E
    kn_g = kn_ref[...]
    krg = kr * kn_g[:, MLA_NOPE:MLA_QPAD]
    k_rot = krg * cos_ref[...] + _swap16(krg) * sin_ref[...]
    k_rot2 = k_rot + pltpu.roll(k_rot, MLA_ROPE, axis=1)
    qtab = qtab_ref[...]
    ones_rows = jnp.ones((VT_ROWS - MLA_DV, ROW_TILE), BF16)

    for h in range(MLA_H):
        q0 = q_all[:, h * MLA_QPAD:h * MLA_QPAD + MLA_NOPE]
        q1 = q_all[:, h * MLA_QPAD + MLA_NOPE:(h + 1) * MLA_QPAD]
        ss = jnp.sum(q0 * q0 + 0.5 * (q1 * q1), axis=-1, keepdims=True)
        inv = lax.rsqrt(ss * (1.0 / MLA_QK) + EPS)
        q_ref[h, :, 0:MLA_NOPE] = (q0 * inv * qn_g[:, 0:MLA_NOPE]).astype(BF16)
        q_ref[h, :, MLA_NOPE:MLA_QPAD] = (q1 * inv * qn_g[:, MLA_NOPE:MLA_QPAD] * qtab).astype(BF16)

        kn = kn_all[:, h * MLA_NOPE:(h + 1) * MLA_NOPE]
        inv = lax.rsqrt((jnp.sum(kn * kn, axis=-1, keepdims=True) + kr_ss) * (1.0 / MLA_QK) + EPS)
        k_ref[h, :, 0:MLA_NOPE] = (kn * inv * kn_g[:, 0:MLA_NOPE]).astype(BF16)
        k_ref[h, :, MLA_NOPE:MLA_QPAD] = (k_rot2 * inv).astype(BF16)
        vt_ref[h, 0:MLA_DV, :] = vt_all[h * MLA_DV:(h + 1) * MLA_DV, :].astype(BF16)
        vt_ref[h, MLA_DV:VT_ROWS, :] = ones_rows


def _mla_prep(u, cos_t, sin_t, qtab_t, wq_p, wk, wvt, gq, gkv, qn_p, kn_p):
    const = lambda i: (0, 0)
    table = pl.BlockSpec((ROW_TILE, 128), lambda i: (i % TILES_PER_BATCH, 0))
    return pl.pallas_call(
        _mla_prep_kernel,
        grid=(N_TILES,),
        in_specs=[
            pl.BlockSpec((ROW_TILE, 512), lambda i: (i, COL_CQKV)),
            pl.BlockSpec((ROW_TILE, 128), lambda i: (i, COL_SMALL)),
            table, table, table,
            pl.BlockSpec((MLA_Q_RANK, MLA_H * MLA_QPAD), const),
            pl.BlockSpec((MLA_KV_RANK, MLA_H * MLA_NOPE), const),
            pl.BlockSpec((MLA_H * MLA_DV, MLA_KV_RANK), const),
            pl.BlockSpec((1, MLA_Q_RANK), const),
            pl.BlockSpec((1, MLA_KV_RANK), const),
            pl.BlockSpec((1, MLA_QPAD), const),
            pl.BlockSpec((1, MLA_QPAD), const),
        ],
        out_specs=[
            pl.BlockSpec((MLA_H, ROW_TILE, MLA_QPAD), lambda i: (0, i, 0)),
            pl.BlockSpec((MLA_H, ROW_TILE, MLA_QPAD), lambda i: (0, i, 0)),
            pl.BlockSpec((MLA_H, VT_ROWS, ROW_TILE), lambda i: (0, 0, i)),
        ],
        out_shape=[
            jax.ShapeDtypeStruct((MLA_H, N_ROWS, MLA_QPAD), BF16),
            jax.ShapeDtypeStruct((MLA_H, N_ROWS, MLA_QPAD), BF16),
            jax.ShapeDtypeStruct((MLA_H, VT_ROWS, N_ROWS), BF16),
        ],
        compiler_params=pltpu.CompilerParams(
            dimension_semantics=("arbitrary",), vmem_limit_bytes=VMEM_LIMIT),
        name="mla_prep",
    )(u, u, cos_t, sin_t, qtab_t, wq_p, wk, wvt, gq, gkv, qn_p, kn_p)


def _attn_kernel(q_ref, k_ref, vt_ref, o_ref):
    def softmax_tile(q, kv_bounds):
        m = acc = None
        scores = lambda b: _dot_nt(k_ref[0, b[0]:b[1], :], q)
        s_next = scores(kv_bounds[0])
        for c, (lo, hi) in enumerate(kv_bounds):
            s = s_next
            if c + 1 < len(kv_bounds):
                s_next = scores(kv_bounds[c + 1])
            m_c = jnp.max(s, axis=0, keepdims=True)
            m_new = m_c if m is None else jnp.maximum(m, m_c)
            p = jnp.exp2(s - m_new)
            pv = _dot(vt_ref[0, :, lo:hi], p.astype(BF16))
            acc = pv if m is None else jnp.exp2(m - m_new) * acc + pv
            m = m_new
        return (acc[0:MLA_DV] / acc[MLA_DV:MLA_DV + 1]).T.astype(o_ref.dtype)

    lat_bounds = [(c * KV_TILE, (c + 1) * KV_TILE) for c in range(SEQ // KV_TILE)] + [(SEQ, SB)]

    def q_step(i, carry):
        r0 = pl.multiple_of(i * Q_TILE, Q_TILE)
        o_ref[pl.ds(r0, Q_TILE), :] = softmax_tile(q_ref[0, pl.ds(r0, Q_TILE), :], lat_bounds)
        return carry

    lax.fori_loop(0, SEQ // Q_TILE, q_step, 0)
    o_ref[SEQ:SB, :] = softmax_tile(q_ref[0, SEQ:SB, :], [(SEQ, SB)])


def _attention(q, k, vt):
    return pl.pallas_call(
        _attn_kernel,
        grid=(BATCH, MLA_H),
        in_specs=[
            pl.BlockSpec((1, SB, MLA_QPAD), lambda b, h: (h, b, 0)),
            pl.BlockSpec((1, SB, MLA_QPAD), lambda b, h: (h, b, 0)),
            pl.BlockSpec((1, VT_ROWS, SB), lambda b, h: (h, 0, b)),
        ],
        out_specs=pl.BlockSpec((SB, MLA_DV), lambda b, h: (b, h)),
        out_shape=jax.ShapeDtypeStruct((N_ROWS, MLA_H * MLA_DV), ACT),
        compiler_params=pltpu.CompilerParams(
            dimension_semantics=("arbitrary", "arbitrary"), vmem_limit_bytes=VMEM_LIMIT),
        name="mla_attention",
    )(q, k, vt)


def _log2_sigmoid(z):
    z2 = z * math.log2(math.e)
    return jnp.minimum(z2, 0.0) - jnp.log2(1.0 + jnp.exp2(-jnp.abs(z2)))


def _gla_kernel(qf_ref, kf_ref, vf_ref, sf_ref, qb_ref, kb_ref, vb_ref, sb_ref, wup_ref, bup_ref,
                of_ref, ob_ref, st_ref):
    @pl.when(pl.program_id(1) == 0)
    def _():
        st_ref[...] = jnp.zeros_like(st_ref)

    hk = GLA_H * GLA_DK
    n_chunks = ROW_TILE // CHUNK
    row = lax.broadcasted_iota(jnp.int32, (ROW_TILE, ROW_TILE), 0)
    col = lax.broadcasted_iota(jnp.int32, (ROW_TILE, ROW_TILE), 1)
    same_chunk = (row // CHUNK) == (col // CHUNK)
    q_scale = GLA_DK ** -0.5

    dirs = ((qf_ref, kf_ref, vf_ref, sf_ref, of_ref), (qb_ref, kb_ref, vb_ref, sb_ref, ob_ref))
    heads = [(d, h) for d in range(2) for h in range(GLA_H)]
    ksl = lambda h: slice(h * GLA_DK, (h + 1) * GLA_DK)
    vsl = lambda h: slice(h * GLA_DV, (h + 1) * GLA_DV)
    csl = lambda c: slice(c * CHUNK, (c + 1) * CHUNK)
    keep = [jnp.logical_and(same_chunk, row >= col), jnp.logical_and(same_chunk, row <= col)]
    scan = [list(range(n_chunks)), list(range(n_chunks - 1, -1, -1))]

    z = [_dot(dirs[d][3][...], wup_ref[:, d * hk:(d + 1) * hk]) + bup_ref[:, d * hk:(d + 1) * hk] for d in range(2)]
    cs = []
    for d in range(2):
        la = _log2_sigmoid(z[d]) * (1.0 / GLA_TAU)
        la_hi = la.astype(BF16)
        la_lo = (la - la_hi.astype(F32)).astype(BF16)
        tri = jnp.where(keep[d], 1.0, 0.0).astype(BF16)
        cs.append(_dot(tri, jnp.concatenate([la_hi, la_lo], axis=1)))

    q_dec, k_inv, k_end, g = [], [], [], []
    for d in range(2):
        q_ref, k_ref = dirs[d][0], dirs[d][1]
        b = cs[d][:, :hk] + cs[d][:, hk:]
        edge = (lambda c: c * CHUNK + CHUNK - 1) if d == 0 else (lambda c: c * CHUNK)
        tots = [b[edge(c):edge(c) + 1] for c in range(n_chunks)]
        tot = jnp.concatenate([jnp.broadcast_to(t, (CHUNK, hk)) for t in tots], axis=0)
        kf = k_ref[...].astype(F32)
        q_dec.append(((q_ref[...].astype(F32) * q_scale) * jnp.exp2(b)).astype(BF16))
        k_inv.append((kf * jnp.exp2(-b)).astype(BF16))
        k_end.append(kf * jnp.exp2(tot - b))
        g.append([jnp.exp2(t) for t in tots])

    att = {(d, h): _dot_nt(q_dec[d][:, ksl(h)], k_inv[d][:, ksl(h)]) for (d, h) in heads}
    upd = {(d, h, c): _dot(k_end[d][csl(c), ksl(h)].T.astype(BF16), dirs[d][2][csl(c), vsl(h)])
           for (d, h) in heads for c in range(n_chunks)}
    o_intra = {(d, h): _dot(jnp.where(keep[d], att[d, h], 0.0).astype(BF16), dirs[d][2][:, vsl(h)])
               for (d, h) in heads}

    states = {}
    for (d, h) in heads:
        st = st_ref[d, h]
        for c in scan[d]:
            states[d, h, c] = st.astype(BF16)
            g_col = jnp.broadcast_to(g[d][c][:, ksl(h)], (GLA_DK, GLA_DK)).T
            st = st * jnp.concatenate([g_col] * (GLA_DV // GLA_DK), axis=1) + upd[d, h, c]
        st_ref[d, h] = st

    for (d, h) in heads:
        o_ref = dirs[d][4]
        for c in range(n_chunks):
            o_inter = _dot(q_dec[d][csl(c), ksl(h)], states[d, h, c])
            o_ref[csl(c), vsl(h)] = (o_intra[d, h][csl(c)] + o_inter).astype(o_ref.dtype)


def _gla(u, wup_p, bup_p):
    def fwd_tile(b, s):
        return b * TILES_PER_BATCH + jnp.where(s == 0, X_TILES, s - 1)

    def bwd_tile(b, s):
        return b * TILES_PER_BATCH + jnp.where(s == 0, X_TILES, X_TILES - s)

    def specs(tile):
        return [
            pl.BlockSpec((ROW_TILE, 512), lambda b, s: (tile(b, s), COL_BQ)),
            pl.BlockSpec((ROW_TILE, 512), lambda b, s: (tile(b, s), COL_BK)),
            pl.BlockSpec((ROW_TILE, 1024), lambda b, s: (tile(b, s), COL_BV)),
            pl.BlockSpec((ROW_TILE, 128), lambda b, s: (tile(b, s), COL_SMALL)),
        ]

    return pl.pallas_call(
        _gla_kernel,
        grid=(BATCH, TILES_PER_BATCH),
        in_specs=specs(fwd_tile) + specs(bwd_tile) + [
            pl.BlockSpec((128, 2 * GLA_H * GLA_DK), lambda b, s: (0, 0)),
            pl.BlockSpec((1, 2 * GLA_H * GLA_DK), lambda b, s: (0, 0)),
        ],
        out_specs=[
            pl.BlockSpec((ROW_TILE, GLA_H * GLA_DV), lambda b, s: (fwd_tile(b, s), 0)),
            pl.BlockSpec((ROW_TILE, GLA_H * GLA_DV), lambda b, s: (bwd_tile(b, s), 0)),
        ],
        out_shape=[jax.ShapeDtypeStruct((N_ROWS, GLA_H * GLA_DV), ACT)] * 2,
        scratch_shapes=[pltpu.VMEM((2, GLA_H, GLA_DK, GLA_DV), F32)],
        compiler_params=pltpu.CompilerParams(
            dimension_semantics=("arbitrary", "arbitrary"), vmem_limit_bytes=VMEM_LIMIT),
        name="gla_bidir",
    )(u, u, u, u, u, u, u, u, wup_p, bup_p)


def _merge_kernel(av_ref, ab_ref, ac_ref, az_ref, bz_ref, cz_ref, ga_ref, gb_ref, gc_ref,
                  avp_ref, acp_ref, avn_ref, acn_ref, of_ref, ob_ref, oc_ref, x_ref, mod_ref,
                  cw_ref, gng_ref, wa_ref, wb_ref, wc_ref, wo_ref, out_ref, *, tiles_per_batch):
    tt = pl.program_id(0) % tiles_per_batch
    prev_ok = jnp.logical_and(tt != 0, tt < X_TILES).astype(F32)
    next_ok = (tt < X_TILES - 1).astype(F32)

    f32 = lambda ref: ref[...].astype(F32)
    p = f32(ac_ref) * f32(av_ref)
    p_prev = f32(acp_ref)[HALO - 1:HALO, :] * f32(avp_ref)[HALO - 1:HALO, :] * prev_ok
    p_next = f32(acn_ref)[0:1, :] * f32(avn_ref)[0:1, :] * next_ok
    rows = lax.broadcasted_iota(jnp.int32, p.shape, 0)
    p_up = jnp.where(rows == 0, p_prev, pltpu.roll(p, 1, axis=0))
    p_dn = jnp.where(rows == ROW_TILE - 1, p_next, pltpu.roll(p, ROW_TILE - 1, axis=0))
    cw = cw_ref[...]
    conv = p_up * cw[0:1] + p * cw[1:2] + p_dn * cw[2:3]
    y_a = f32(ab_ref) * conv * _silu(f32(az_ref))
    m = _sigmoid(f32(ga_ref)) * _dot(y_a.astype(BF16), wa_ref[...])

    o = f32(of_ref) + f32(ob_ref)
    gng = gng_ref[...]
    y_b = []
    for h in range(GLA_H):
        sl = slice(h * GLA_DV, (h + 1) * GLA_DV)
        oh = o[:, sl]
        y_b.append(oh * lax.rsqrt(jnp.mean(oh * oh, axis=-1, keepdims=True) + EPS) * gng[:, sl])
    y_b = jnp.concatenate(y_b, axis=-1) * _silu(f32(bz_ref))
    m = m + _sigmoid(f32(gb_ref)) * _dot(y_b.astype(BF16), wb_ref[...])

    y_c = f32(oc_ref) * _silu(f32(cz_ref))
    m = m + _sigmoid(f32(gc_ref)) * _dot(y_c.astype(BF16), wc_ref[...])

    gate = mod_ref[0][:, 2 * D:3 * D]
    out_ref[...] = x_ref[...] + gate * _dot(m.astype(BF16), wo_ref[...])


def _merge(u, o_f, o_b, o_c, xs, mod_l, conv_w, gla_norm_g, wa, wb, wc, wo, last):
    halo_blocks = ROW_TILE // HALO
    n_halo = N_ROWS // HALO
    if last:
        grid = (BATCH * X_TILES,)
        tile = lambda i: (i // X_TILES) * TILES_PER_BATCH + i % X_TILES
        out_rows = BATCH * SEQ
        out_map = lambda i: (i, 0)
    else:
        grid = (N_TILES,)
        tile = lambda i: i
        out_rows = N_ROWS
        out_map = lambda i: (i, 0)

    def col(c):
        return pl.BlockSpec((ROW_TILE, D), lambda i: (tile(i), c))

    def prev_rows(c):
        return pl.BlockSpec((HALO, D), lambda i: (jnp.maximum(tile(i) * halo_blocks - 1, 0), c))

    def next_rows(c):
        return pl.BlockSpec((HALO, D), lambda i: (jnp.minimum((tile(i) + 1) * halo_blocks, n_halo - 1), c))

    row_block = pl.BlockSpec((ROW_TILE, D), lambda i: (tile(i), 0))
    const = lambda i: (0, 0)
    weight = pl.BlockSpec((D, D), const)

    return pl.pallas_call(
        functools.partial(_merge_kernel, tiles_per_batch=X_TILES if last else TILES_PER_BATCH),
        grid=grid,
        in_specs=[col(COL_AV), col(COL_AB), col(COL_AC), col(COL_AZ), col(COL_BZ), col(COL_CZ),
                  col(COL_GA), col(COL_GB), col(COL_GC),
                  prev_rows(COL_AV), prev_rows(COL_AC), next_rows(COL_AV), next_rows(COL_AC),
                  row_block, row_block, row_block, row_block,
                  pl.BlockSpec((1, 1, 3 * D), lambda i: (_tile_group(tile(i)), 0, 0)),
                  pl.BlockSpec((3, D), const), pl.BlockSpec((1, D), const),
                  weight, weight, weight, weight],
        out_specs=pl.BlockSpec((ROW_TILE, D), out_map),
        out_shape=jax.ShapeDtypeStruct((out_rows, D), F32),
        compiler_params=pltpu.CompilerParams(
            dimension_semantics=("arbitrary",), vmem_limit_bytes=VMEM_LIMIT),
        name="merge_last" if last else "merge",
    )(u, u, u, u, u, u, u, u, u, u, u, u, u, o_f, o_b, o_c, xs, mod_l, conv_w, gla_norm_g,
      wa, wb, wc, wo)


def _rope_tables():
    t = np.arange(SEQ)
    row = (t // GRID_W).astype(np.float32)
    colp = (t % GRID_W).astype(np.float32)
    n_freq = MLA_ROPE // 4
    freqs = (np.float32(ROPE_BASE) ** (-np.arange(n_freq, dtype=np.float32) / np.float32(n_freq))).astype(np.float32)
    ang_r = row[:, None] * freqs[None, :]
    ang_c = colp[:, None] * freqs[None, :]
    ang = np.concatenate([ang_r, ang_r, ang_c, ang_c], axis=-1).astype(np.float32)
    cos = np.ones((SB, 128), np.float32)
    sin = np.zeros((SB, 128), np.float32)
    cos[:SEQ, :MLA_ROPE] = np.cos(ang)
    sign = np.where((np.arange(MLA_ROPE) // n_freq) % 2 == 0, -1.0, 1.0).astype(np.float32)
    sin[:SEQ, :MLA_ROPE] = np.sin(ang) * sign[None, :]
    qtab = np.concatenate([cos[:, :MLA_ROPE], sin[:, :MLA_ROPE]], axis=1)
    return jnp.asarray(cos), jnp.asarray(sin), jnp.asarray(qtab)


def _swap16_index():
    l = np.arange(MLA_ROPE)
    return np.where((l // 16) % 2 == 0, l + 16, l - 16)


def _permute_w_in(w):
    seg = lambda a, n: w[:, a:a + n]
    parts = [
        seg(0, 4096),
        seg(5120, 1024), seg(6144, 1024),
        seg(7776, 1024),
        seg(8800, 3072),
        seg(4096, 512), seg(4608, 512),
        seg(7200, 384), seg(7584, 128),
        seg(7712, 64), seg(7168, 16), seg(7184, 16),
    ]
    used = sum(p.shape[1] for p in parts)
    parts.append(jnp.zeros((D, IN_PAD - used), w.dtype))
    return jnp.concatenate(parts, axis=1).astype(BF16)


def kernel(x, c, ctx, c_ctx, w_mod, b_mod, norm_g, w_in, conv_w, gla_wa_up_f, gla_ba_f, gla_wa_up_b,
           gla_ba_b, gla_norm_g, mla_q_norm_g, mla_kv_norm_g, mla_wq_up, mla_wkv_up, mla_qn_g, mla_kn_g,
           w_br_a, w_br_b, w_br_c, w_out):
    cos_t, sin_t, qtab_t = _rope_tables()
    perm = _swap16_index()
    xs = jnp.concatenate([x, ctx], axis=1).reshape(N_ROWS, D)
    cc = jnp.concatenate([c, c_ctx[None, :], jnp.zeros((8 - BATCH - 1, D), F32)], axis=0)
    mods = _modulation(cc, w_mod, b_mod).reshape(DEPTH, 8, 1, 3 * D)

    hk = GLA_H * GLA_DK
    for l in range(DEPTH):
        last = l == DEPTH - 1
        w_in_p = _permute_w_in(w_in[l])
        wup_p = jnp.zeros((128, 2 * hk), F32)
        wup_p = wup_p.at[SMALL_AF:SMALL_AF + GLA_RANK, :hk].set(gla_wa_up_f[l])
        wup_p = wup_p.at[SMALL_AB:SMALL_AB + GLA_RANK, hk:].set(gla_wa_up_b[l]).astype(BF16)
        bup_p = jnp.concatenate([gla_ba_f[l], gla_ba_b[l]])[None, :]
        wq3 = mla_wq_up[l].reshape(MLA_Q_RANK, MLA_H, MLA_QK)
        wq_rope = wq3[:, :, MLA_NOPE:]
        wq_p = jnp.concatenate([wq3, wq_rope[:, :, perm]], axis=2).reshape(MLA_Q_RANK, MLA_H * MLA_QPAD).astype(BF16)
        qn_rope = mla_qn_g[l][MLA_NOPE:]
        qn_p = jnp.concatenate([mla_qn_g[l], qn_rope[perm]])[None, :]
        kn_p = jnp.pad(mla_kn_g[l], (0, MLA_QPAD - MLA_QK))[None, :]
        wkv3 = mla_wkv_up[l].reshape(MLA_KV_RANK, MLA_H, MLA_NOPE + MLA_DV)
        wk = wkv3[:, :, :MLA_NOPE].reshape(MLA_KV_RANK, MLA_H * MLA_NOPE).astype(BF16)
        wvt = wkv3[:, :, MLA_NOPE:].reshape(MLA_KV_RANK, MLA_H * MLA_DV).T.astype(BF16)

        u = _inproj(xs, mods[l], norm_g[l][None, :], w_in_p)
        q, k, vt = _mla_prep(u, cos_t, sin_t, qtab_t, wq_p, wk, wvt,
                             mla_q_norm_g[l][None, :], mla_kv_norm_g[l][None, :], qn_p, kn_p)
        o_c = _attention(q, k, vt)
        o_f, o_b = _gla(u, wup_p, bup_p)
        xs = _merge(u, o_f, o_b, o_c, xs, mods[l], conv_w[l], gla_norm_g[l][None, :],
                    w_br_a[l].astype(BF16), w_br_b[l].astype(BF16), w_br_c[l].astype(BF16),
                    w_out[l].astype(BF16), last)
    return xs.reshape(BATCH, SEQ, D)
```

```python
import functools
import math

import numpy as np
import jax
import jax.numpy as jnp
from jax import lax
from jax.experimental import pallas as pl
from jax.experimental.pallas import tpu as pltpu

D = 1024
BATCH = 4
SEQ = 4096
DEPTH = 4
GRID_W = 64
CTX = 256
EPS = 1e-6

GLA_H = 4
GLA_DK = 128
GLA_DV = 256
GLA_RANK = 16
GLA_TAU = 16.0
CHUNK = 64

MLA_H = 8
MLA_NOPE = 128
MLA_ROPE = 64
MLA_DV = 128
MLA_QK = MLA_NOPE + MLA_ROPE
MLA_QPAD = 256
VT_ROWS = MLA_DV + 16
_SM_SCALE_LOG2E = (MLA_QK ** -0.5) * math.log2(math.e)
MLA_Q_RANK = 384
MLA_KV_RANK = 128
ROPE_BASE = 10000.0

SB = SEQ + CTX
N_ROWS = BATCH * SB
ROW_TILE = 256
TILES_PER_BATCH = SB // ROW_TILE
X_TILES = SEQ // ROW_TILE
N_TILES = N_ROWS // ROW_TILE
CTX_GROUP = BATCH

IN_PAD = 12288
IN_HEAD_COLS = 7168
COL_AV, COL_AB, COL_AC, COL_AZ, _, COL_BV, COL_BZ, COL_CZ, COL_GA, COL_GB, COL_GC = range(11)
COL_BQ, COL_BK, COL_CQKV = 8, 9, 22
COL_SMALL = 92
SMALL_KR, SMALL_AF, SMALL_AB = 0, 64, 80

IN_TN = 1024
IN_HEAD_TILES = IN_HEAD_COLS // IN_TN
IN_TILES_PER_BATCH = 4
IN_TM = SB // IN_TILES_PER_BATCH
Q_TILE = 1024
KV_TILE = 512
ATTN_LOOKAHEAD = 2

VMEM_LIMIT = 56 * 1024 * 1024

F32 = jnp.float32
BF16 = jnp.bfloat16
ACT = BF16
HALO = 16


def _dot(a, b):
    return jnp.dot(a, b, preferred_element_type=F32)


def _dot_nt(a, b):
    return lax.dot_general(a, b, (((1,), (1,)), ((), ())), preferred_element_type=F32)


def _dot_exact(a, b):
    return jnp.dot(a, b, preferred_element_type=F32, precision=lax.Precision.HIGHEST)


def _sigmoid(x):
    return 1.0 / (1.0 + jnp.exp2(x * -math.log2(math.e)))


def _silu(x):
    return x * _sigmoid(x)


def _tile_group(i):
    tt = i % TILES_PER_BATCH
    return jnp.where(tt >= X_TILES, CTX_GROUP, i // TILES_PER_BATCH)


def _mod_kernel(cc_ref, w_ref, b_ref, o_ref):
    o_ref[0] = _dot_exact(_silu(cc_ref[...]), w_ref[0]) + b_ref[0]


def _modulation(cc, w_mod, b_mod):
    return pl.pallas_call(
        _mod_kernel,
        grid=(DEPTH,),
        in_specs=[
            pl.BlockSpec((8, D), lambda l: (0, 0)),
            pl.BlockSpec((1, D, 3 * D), lambda l: (l, 0, 0)),
            pl.BlockSpec((1, 1, 3 * D), lambda l: (l, 0, 0)),
        ],
        out_specs=pl.BlockSpec((1, 8, 3 * D), lambda l: (l, 0, 0)),
        out_shape=jax.ShapeDtypeStruct((DEPTH, 8, 3 * D), F32),
        compiler_params=pltpu.CompilerParams(
            dimension_semantics=("arbitrary",), vmem_limit_bytes=VMEM_LIMIT),
        name="modulation",
    )(cc, w_mod, b_mod.reshape(DEPTH, 1, 3 * D))


def _inproj_kernel(x_ref, modx_ref, modc_ref, g_ref, wh_ref, wt_ref, u_ref, h_ref):
    @pl.when(pl.program_id(1) == 0)
    def _():
        x = x_ref[...]
        y = x * lax.rsqrt(jnp.mean(x * x, axis=-1, keepdims=True) + EPS) * g_ref[...]
        row = (pl.program_id(0) % IN_TILES_PER_BATCH) * IN_TM + lax.broadcasted_iota(jnp.int32, x.shape, 0)
        is_ctx = row >= SEQ
        modx = modx_ref[0]
        modc = modc_ref[0]
        scale = jnp.where(is_ctx, modc[:, D:2 * D], modx[:, D:2 * D])
        shift = jnp.where(is_ctx, modc[:, 0:D], modx[:, 0:D])
        h_ref[...] = (y * (1.0 + scale) + shift).astype(BF16)

    @pl.when(pl.program_id(1) < IN_HEAD_TILES)
    def _():
        u_ref[...] = _dot(h_ref[...], wh_ref[...].astype(BF16)).astype(u_ref.dtype)

    @pl.when(pl.program_id(1) >= IN_HEAD_TILES)
    def _():
        u_ref[...] = _dot(h_ref[...], wt_ref[...]).astype(u_ref.dtype)


def _inproj(xs, mod_l, norm_g, w_in, layer, w_tail):
    return pl.pallas_call(
        _inproj_kernel,
        grid=(N_ROWS // IN_TM, IN_PAD // IN_TN),
        in_specs=[
            pl.BlockSpec((IN_TM, D), lambda i, j: (i, 0)),
            pl.BlockSpec((1, 1, 3 * D), lambda i, j: (i // IN_TILES_PER_BATCH, 0, 0)),
            pl.BlockSpec((1, 1, 3 * D), lambda i, j: (CTX_GROUP, 0, 0)),
            pl.BlockSpec((1, D), lambda i, j: (0, 0)),
            pl.BlockSpec((None, D, IN_TN), lambda i, j: (layer, 0, jnp.minimum(j, IN_HEAD_TILES - 1))),
            pl.BlockSpec((None, D, IN_TN), lambda i, j: (layer, 0, jnp.maximum(j - IN_HEAD_TILES, 0))),
        ],
        out_specs=pl.BlockSpec((IN_TM, IN_TN), lambda i, j: (i, j)),
        out_shape=jax.ShapeDtypeStruct((N_ROWS, IN_PAD), ACT),
        scratch_shapes=[pltpu.VMEM((IN_TM, D), BF16)],
        compiler_params=pltpu.CompilerParams(
            dimension_semantics=("arbitrary", "arbitrary"), vmem_limit_bytes=VMEM_LIMIT),
        name="inproj",
    )(xs, mod_l, mod_l, norm_g, w_in, w_tail)


def _swap16(r):
    lane = lax.broadcasted_iota(jnp.int32, r.shape, 1)
    even = ((lane // 16) % 2) == 0
    return jnp.where(even, pltpu.roll(r, 112, axis=1), pltpu.roll(r, 16, axis=1))


def _mla_prep_kernel(cqkv_ref, small_ref, cos_ref, sin_ref, qtab_ref, wq_ref, wk_ref, wvt_ref, gq_ref, gkv_ref,
                     qn_ref, kn_ref, q_ref, k_ref, vt_ref):
    cqkv = cqkv_ref[...].astype(F32)
    cq = cqkv[:, :MLA_Q_RANK]
    ckv = cqkv[:, MLA_Q_RANK:]

    cq_n = cq * lax.rsqrt(jnp.mean(cq * cq, axis=-1, keepdims=True) + EPS) * gq_ref[...]
    q_all = _dot(cq_n.astype(BF16), wq_ref[...])
    ckv_n = (ckv * lax.rsqrt(jnp.mean(ckv * ckv, axis=-1, keepdims=True) + EPS) * gkv_ref[...]).astype(BF16)
    kn_all = _dot(ckv_n, wk_ref[...])
    vt_all = _dot_nt(wvt_ref[...], ckv_n)

    small = small_ref[...].astype(F32)
    lane = lax.broadcasted_iota(jnp.int32, small.shape, 1)
    kr = jnp.where(lane < MLA_ROPE, small, 0.0)
    kr_ss = jnp.sum(kr * kr, axis=-1, keepdims=True)
    qn_g = qn_ref[...] * _SM_SCALE_LOG2E
    kn_g = kn_ref[...]
    krg = kr * kn_g[:, MLA_NOPE:MLA_QPAD]
    k_rot = krg * cos_ref[...] + _swap16(krg) * sin_ref[...]
    k_rot2 = k_rot + pltpu.roll(k_rot, MLA_ROPE, axis=1)
    qtab = qtab_ref[...]
    ones_rows = jnp.ones((VT_ROWS - MLA_DV, ROW_TILE), BF16)

    for h in range(MLA_H):
        q0 = q_all[:, h * MLA_QPAD:h * MLA_QPAD + MLA_NOPE]
        q1 = q_all[:, h * MLA_QPAD + MLA_NOPE:(h + 1) * MLA_QPAD]
        ss = jnp.sum(q0 * q0 + 0.5 * (q1 * q1), axis=-1, keepdims=True)
        inv = lax.rsqrt(ss * (1.0 / MLA_QK) + EPS)
        q_ref[h, :, 0:MLA_NOPE] = (q0 * inv * qn_g[:, 0:MLA_NOPE]).astype(BF16)
        q_ref[h, :, MLA_NOPE:MLA_QPAD] = (q1 * inv * qn_g[:, MLA_NOPE:MLA_QPAD] * qtab).astype(BF16)

        kn = kn_all[:, h * MLA_NOPE:(h + 1) * MLA_NOPE]
        inv = lax.rsqrt((jnp.sum(kn * kn, axis=-1, keepdims=True) + kr_ss) * (1.0 / MLA_QK) + EPS)
        k_ref[h, :, 0:MLA_NOPE] = (kn * inv * kn_g[:, 0:MLA_NOPE]).astype(BF16)
        k_ref[h, :, MLA_NOPE:MLA_QPAD] = (k_rot2 * inv).astype(BF16)
        vt_ref[h, 0:MLA_DV, :] = vt_all[h * MLA_DV:(h + 1) * MLA_DV, :].astype(BF16)
        vt_ref[h, MLA_DV:VT_ROWS, :] = ones_rows


def _mla_prep(u, cos_t, sin_t, qtab_t, wq_p, wk, wvt, gq, gkv, qn_p, kn_p):
    const = lambda i: (0, 0)
    table = pl.BlockSpec((ROW_TILE, 128), lambda i: (i % TILES_PER_BATCH, 0))
    return pl.pallas_call(
        _mla_prep_kernel,
        grid=(N_TILES,),
        in_specs=[
            pl.BlockSpec((ROW_TILE, 512), lambda i: (i, COL_CQKV)),
            pl.BlockSpec((ROW_TILE, 128), lambda i: (i, COL_SMALL)),
            table, table, table,
            pl.BlockSpec((MLA_Q_RANK, MLA_H * MLA_QPAD), const),
            pl.BlockSpec((MLA_KV_RANK, MLA_H * MLA_NOPE), const),
            pl.BlockSpec((MLA_H * MLA_DV, MLA_KV_RANK), const),
            pl.BlockSpec((1, MLA_Q_RANK), const),
            pl.BlockSpec((1, MLA_KV_RANK), const),
            pl.BlockSpec((1, MLA_QPAD), const),
            pl.BlockSpec((1, MLA_QPAD), const),
        ],
        out_specs=[
            pl.BlockSpec((MLA_H, ROW_TILE, MLA_QPAD), lambda i: (0, i, 0)),
            pl.BlockSpec((MLA_H, ROW_TILE, MLA_QPAD), lambda i: (0, i, 0)),
            pl.BlockSpec((MLA_H, VT_ROWS, ROW_TILE), lambda i: (0, 0, i)),
        ],
        out_shape=[
            jax.ShapeDtypeStruct((MLA_H, N_ROWS, MLA_QPAD), BF16),
            jax.ShapeDtypeStruct((MLA_H, N_ROWS, MLA_QPAD), BF16),
            jax.ShapeDtypeStruct((MLA_H, VT_ROWS, N_ROWS), BF16),
        ],
        compiler_params=pltpu.CompilerParams(
            dimension_semantics=("arbitrary",), vmem_limit_bytes=VMEM_LIMIT),
        name="mla_prep",
    )(u, u, cos_t, sin_t, qtab_t, wq_p, wk, wvt, gq, gkv, qn_p, kn_p)


def _attn_kernel(q_ref, k_ref, vt_ref, o_ref):
    def softmax_tile(q, kv_bounds):
        m = acc = None
        scores = lambda b: _dot_nt(k_ref[0, b[0]:b[1], :], q)
        ahead = [scores(b) for b in kv_bounds[:ATTN_LOOKAHEAD]]
        for c, (lo, hi) in enumerate(kv_bounds):
            s = ahead.pop(0)
            if c + ATTN_LOOKAHEAD < len(kv_bounds):
                ahead.append(scores(kv_bounds[c + ATTN_LOOKAHEAD]))
            m_c = jnp.max(s, axis=0, keepdims=True)
            m_new = m_c if m is None else jnp.maximum(m, m_c)
            p = jnp.exp2(s - m_new)
            pv = _dot(vt_ref[0, :, lo:hi], p.astype(BF16))
            acc = pv if m is None else jnp.exp2(m - m_new) * acc + pv
            m = m_new
        return (acc[0:MLA_DV] / acc[MLA_DV:MLA_DV + 1]).T.astype(o_ref.dtype)

    n_kv = SEQ // KV_TILE
    lat_bounds = [(c * KV_TILE, (c + 1) * KV_TILE) for c in range(n_kv - 1)] + [((n_kv - 1) * KV_TILE, SB)]

    def q_step(i, carry):
        r0 = pl.multiple_of(i * Q_TILE, Q_TILE)
        o_ref[pl.ds(r0, Q_TILE), :] = softmax_tile(q_ref[0, pl.ds(r0, Q_TILE), :], lat_bounds)
        return carry

    lax.fori_loop(0, SEQ // Q_TILE, q_step, 0)
    o_ref[SEQ:SB, :] = softmax_tile(q_ref[0, SEQ:SB, :], [(SEQ, SB)])


def _attention(q, k, vt):
    return pl.pallas_call(
        _attn_kernel,
        grid=(BATCH, MLA_H),
        in_specs=[
            pl.BlockSpec((1, SB, MLA_QPAD), lambda b, h: (h, b, 0)),
            pl.BlockSpec((1, SB, MLA_QPAD), lambda b, h: (h, b, 0)),
            pl.BlockSpec((1, VT_ROWS, SB), lambda b, h: (h, 0, b)),
        ],
        out_specs=pl.BlockSpec((SB, MLA_DV), lambda b, h: (b, h)),
        out_shape=jax.ShapeDtypeStruct((N_ROWS, MLA_H * MLA_DV), ACT),
        compiler_params=pltpu.CompilerParams(
            dimension_semantics=("arbitrary", "arbitrary"), vmem_limit_bytes=VMEM_LIMIT),
        name="mla_attention",
    )(q, k, vt)


def _log2_sigmoid(z):
    z2 = z * math.log2(math.e)
    return jnp.minimum(z2, 0.0) - jnp.log2(1.0 + jnp.exp2(-jnp.abs(z2)))


def _gla_kernel(qf_ref, kf_ref, vf_ref, sf_ref, qb_ref, kb_ref, vb_ref, sb_ref, wup_ref, bup_ref,
                of_ref, ob_ref, st_ref):
    @pl.when(pl.program_id(1) == 0)
    def _():
        st_ref[...] = jnp.zeros_like(st_ref)

    hk = GLA_H * GLA_DK
    n_chunks = ROW_TILE // CHUNK
    row = lax.broadcasted_iota(jnp.int32, (ROW_TILE, ROW_TILE), 0)
    col = lax.broadcasted_iota(jnp.int32, (ROW_TILE, ROW_TILE), 1)
    same_chunk = (row // CHUNK) == (col // CHUNK)
    q_scale = GLA_DK ** -0.5

    dirs = ((qf_ref, kf_ref, vf_ref, sf_ref, of_ref), (qb_ref, kb_ref, vb_ref, sb_ref, ob_ref))
    heads = [(d, h) for d in range(2) for h in range(GLA_H)]
    ksl = lambda h: slice(h * GLA_DK, (h + 1) * GLA_DK)
    vsl = lambda h: slice(h * GLA_DV, (h + 1) * GLA_DV)
    csl = lambda c: slice(c * CHUNK, (c + 1) * CHUNK)
    keep = [jnp.logical_and(same_chunk, row >= col), jnp.logical_and(same_chunk, row <= col)]
    scan = [list(range(n_chunks)), list(range(n_chunks - 1, -1, -1))]

    z = [_dot(dirs[d][3][...], wup_ref[:, d * hk:(d + 1) * hk]) + bup_ref[:, d * hk:(d + 1) * hk] for d in range(2)]
    cs = []
    for d in range(2):
        la = _log2_sigmoid(z[d]) * (1.0 / GLA_TAU)
        la_hi = la.astype(BF16)
        la_lo = (la - la_hi.astype(F32)).astype(BF16)
        tri = jnp.where(keep[d], 1.0, 0.0).astype(BF16)
        cs.append(_dot(tri, jnp.concatenate([la_hi, la_lo], axis=1)))

    q_dec, k_inv, k_end, g = [], [], [], []
    for d in range(2):
        q_ref, k_ref = dirs[d][0], dirs[d][1]
        b = cs[d][:, :hk] + cs[d][:, hk:]
        edge = (lambda c: c * CHUNK + CHUNK - 1) if d == 0 else (lambda c: c * CHUNK)
        tots = [b[edge(c):edge(c) + 1] for c in range(n_chunks)]
        tot = jnp.concatenate([jnp.broadcast_to(t, (CHUNK, hk)) for t in tots], axis=0)
        kf = k_ref[...].astype(F32)
        q_dec.append(((q_ref[...].astype(F32) * q_scale) * jnp.exp2(b)).astype(BF16))
        k_inv.append((kf * jnp.exp2(-b)).astype(BF16))
        k_end.append(kf * jnp.exp2(tot - b))
        g.append([jnp.exp2(t) for t in tots])

    att = {(d, h): _dot_nt(q_dec[d][:, ksl(h)], k_inv[d][:, ksl(h)]) for (d, h) in heads}
    upd = {(d, h, c): _dot(k_end[d][csl(c), ksl(h)].T.astype(BF16), dirs[d][2][csl(c), vsl(h)])
           for (d, h) in heads for c in range(n_chunks)}
    o_intra = {(d, h): _dot(jnp.where(keep[d], att[d, h], 0.0).astype(BF16), dirs[d][2][:, vsl(h)])
               for (d, h) in heads}

    states = {}
    for (d, h) in heads:
        st = st_ref[d, h]
        for c in scan[d]:
            states[d, h, c] = st.astype(BF16)
            g_col = jnp.broadcast_to(g[d][c][:, ksl(h)], (GLA_DK, GLA_DK)).T
            st = st * jnp.concatenate([g_col] * (GLA_DV // GLA_DK), axis=1) + upd[d, h, c]
        st_ref[d, h] = st

    for (d, h) in heads:
        o_ref = dirs[d][4]
        for c in range(n_chunks):
            o_inter = _dot(q_dec[d][csl(c), ksl(h)], states[d, h, c])
            o_ref[csl(c), vsl(h)] = (o_intra[d, h][csl(c)] + o_inter).astype(o_ref.dtype)


def _gla(u, wup_p, bup_p):
    def fwd_tile(b, s):
        return b * TILES_PER_BATCH + jnp.where(s == 0, X_TILES, s - 1)

    def bwd_tile(b, s):
        return b * TILES_PER_BATCH + jnp.where(s == 0, X_TILES, X_TILES - s)

    def specs(tile):
        return [
            pl.BlockSpec((ROW_TILE, 512), lambda b, s: (tile(b, s), COL_BQ)),
            pl.BlockSpec((ROW_TILE, 512), lambda b, s: (tile(b, s), COL_BK)),
            pl.BlockSpec((ROW_TILE, 1024), lambda b, s: (tile(b, s), COL_BV)),
            pl.BlockSpec((ROW_TILE, 128), lambda b, s: (tile(b, s), COL_SMALL)),
        ]

    return pl.pallas_call(
        _gla_kernel,
        grid=(BATCH, TILES_PER_BATCH),
        in_specs=specs(fwd_tile) + specs(bwd_tile) + [
            pl.BlockSpec((128, 2 * GLA_H * GLA_DK), lambda b, s: (0, 0)),
            pl.BlockSpec((1, 2 * GLA_H * GLA_DK), lambda b, s: (0, 0)),
        ],
        out_specs=[
            pl.BlockSpec((ROW_TILE, GLA_H * GLA_DV), lambda b, s: (fwd_tile(b, s), 0)),
            pl.BlockSpec((ROW_TILE, GLA_H * GLA_DV), lambda b, s: (bwd_tile(b, s), 0)),
        ],
        out_shape=[jax.ShapeDtypeStruct((N_ROWS, GLA_H * GLA_DV), ACT)] * 2,
        scratch_shapes=[pltpu.VMEM((2, GLA_H, GLA_DK, GLA_DV), F32)],
        compiler_params=pltpu.CompilerParams(
            dimension_semantics=("arbitrary", "arbitrary"), vmem_limit_bytes=VMEM_LIMIT),
        name="gla_bidir",
    )(u, u, u, u, u, u, u, u, wup_p, bup_p)


def _merge_kernel(av_ref, ab_ref, ac_ref, az_ref, bz_ref, cz_ref, ga_ref, gb_ref, gc_ref,
                  avp_ref, acp_ref, avn_ref, acn_ref, of_ref, ob_ref, oc_ref, x_ref, mod_ref,
                  cw_ref, gng_ref, wa_ref, wb_ref, wc_ref, wo_ref, out_ref, *, tiles_per_batch):
    tt = pl.program_id(0) % tiles_per_batch
    prev_ok = jnp.logical_and(tt != 0, tt < X_TILES).astype(F32)
    next_ok = (tt < X_TILES - 1).astype(F32)

    f32 = lambda ref: ref[...].astype(F32)
    p = f32(ac_ref) * f32(av_ref)
    p_prev = f32(acp_ref)[HALO - 1:HALO, :] * f32(avp_ref)[HALO - 1:HALO, :] * prev_ok
    p_next = f32(acn_ref)[0:1, :] * f32(avn_ref)[0:1, :] * next_ok
    rows = lax.broadcasted_iota(jnp.int32, p.shape, 0)
    p_up = jnp.where(rows == 0, p_prev, pltpu.roll(p, 1, axis=0))
    p_dn = jnp.where(rows == ROW_TILE - 1, p_next, pltpu.roll(p, ROW_TILE - 1, axis=0))
    cw = cw_ref[...]
    conv = p_up * cw[0:1] + p * cw[1:2] + p_dn * cw[2:3]
    y_a = f32(ab_ref) * conv * _silu(f32(az_ref))
    m = _sigmoid(f32(ga_ref)) * _dot(y_a.astype(BF16), wa_ref[...])

    o = f32(of_ref) + f32(ob_ref)
    gng = gng_ref[...]
    y_b = []
    for h in range(GLA_H):
        sl = slice(h * GLA_DV, (h + 1) * GLA_DV)
        oh = o[:, sl]
        y_b.append(oh * lax.rsqrt(jnp.mean(oh * oh, axis=-1, keepdims=True) + EPS) * gng[:, sl])
    y_b = jnp.concatenate(y_b, axis=-1) * _silu(f32(bz_ref))
    m = m + _sigmoid(f32(gb_ref)) * _dot(y_b.astype(BF16), wb_ref[...])

    y_c = f32(oc_ref) * _silu(f32(cz_ref))
    m = m + _sigmoid(f32(gc_ref)) * _dot(y_c.astype(BF16), wc_ref[...])

    gate = mod_ref[0][:, 2 * D:3 * D]
    out_ref[...] = x_ref[...] + gate * _dot(m.astype(BF16), wo_ref[...])


def _merge(u, o_f, o_b, o_c, xs, mod_l, conv_w, gla_norm_g, wa, wb, wc, wo, layer, last):
    halo_blocks = ROW_TILE // HALO
    n_halo = N_ROWS // HALO
    if last:
        grid = (BATCH * X_TILES,)
        tile = lambda i: (i // X_TILES) * TILES_PER_BATCH + i % X_TILES
        out_rows = BATCH * SEQ
        out_map = lambda i: (i, 0)
    else:
        grid = (N_TILES,)
        tile = lambda i: i
        out_rows = N_ROWS
        out_map = lambda i: (i, 0)

    def col(c):
        return pl.BlockSpec((ROW_TILE, D), lambda i: (tile(i), c))

    def prev_rows(c):
        return pl.BlockSpec((HALO, D), lambda i: (jnp.maximum(tile(i) * halo_blocks - 1, 0), c))

    def next_rows(c):
        return pl.BlockSpec((HALO, D), lambda i: (jnp.minimum((tile(i) + 1) * halo_blocks, n_halo - 1), c))

    row_block = pl.BlockSpec((ROW_TILE, D), lambda i: (tile(i), 0))
    const = lambda i: (0, 0)
    weight = pl.BlockSpec((None, D, D), lambda i: (layer, 0, 0))

    return pl.pallas_call(
        functools.partial(_merge_kernel, tiles_per_batch=X_TILES if last else TILES_PER_BATCH),
        grid=grid,
        in_specs=[col(COL_AV), col(COL_AB), col(COL_AC), col(COL_AZ), col(COL_BZ), col(COL_CZ),
                  col(COL_GA), col(COL_GB), col(COL_GC),
                  prev_rows(COL_AV), prev_rows(COL_AC), next_rows(COL_AV), next_rows(COL_AC),
                  row_block, row_block, row_block, row_block,
                  pl.BlockSpec((1, 1, 3 * D), lambda i: (_tile_group(tile(i)), 0, 0)),
                  pl.BlockSpec((3, D), const), pl.BlockSpec((1, D), const),
                  weight, weight, weight, weight],
        out_specs=pl.BlockSpec((ROW_TILE, D), out_map),
        out_shape=jax.ShapeDtypeStruct((out_rows, D), F32),
        compiler_params=pltpu.CompilerParams(
            dimension_semantics=("arbitrary",), vmem_limit_bytes=VMEM_LIMIT),
        name="merge_last" if last else "merge",
    )(u, u, u, u, u, u, u, u, u, u, u, u, u, o_f, o_b, o_c, xs, mod_l, conv_w, gla_norm_g,
      wa, wb, wc, wo)


def _rope_tables():
    t = np.arange(SEQ)
    row = (t // GRID_W).astype(np.float32)
    colp = (t % GRID_W).astype(np.float32)
    n_freq = MLA_ROPE // 4
    freqs = (np.float32(ROPE_BASE) ** (-np.arange(n_freq, dtype=np.float32) / np.float32(n_freq))).astype(np.float32)
    ang_r = row[:, None] * freqs[None, :]
    ang_c = colp[:, None] * freqs[None, :]
    ang = np.concatenate([ang_r, ang_r, ang_c, ang_c], axis=-1).astype(np.float32)
    cos = np.ones((SB, 128), np.float32)
    sin = np.zeros((SB, 128), np.float32)
    cos[:SEQ, :MLA_ROPE] = np.cos(ang)
    sign = np.where((np.arange(MLA_ROPE) // n_freq) % 2 == 0, -1.0, 1.0).astype(np.float32)
    sin[:SEQ, :MLA_ROPE] = np.sin(ang) * sign[None, :]
    qtab = np.concatenate([cos[:, :MLA_ROPE], sin[:, :MLA_ROPE]], axis=1)
    return jnp.asarray(cos), jnp.asarray(sin), jnp.asarray(qtab)


def _swap16_index():
    l = np.arange(MLA_ROPE)
    return np.where((l // 16) % 2 == 0, l + 16, l - 16)


def _w_in_tail(w_in):
    seg = lambda a, n: w_in[:, :, a:a + n]
    parts = [
        seg(7776, 1024),
        seg(8800, 3072),
        seg(7200, 384), seg(7584, 128),
        seg(7712, 64), seg(7168, 16), seg(7184, 16),
    ]
    used = sum(p.shape[2] for p in parts)
    parts.append(jnp.zeros((DEPTH, D, IN_PAD - IN_HEAD_COLS - used), w_in.dtype))
    return jnp.concatenate(parts, axis=2).astype(BF16)


def kernel(x, c, ctx, c_ctx, w_mod, b_mod, norm_g, w_in, conv_w, gla_wa_up_f, gla_ba_f, gla_wa_up_b,
           gla_ba_b, gla_norm_g, mla_q_norm_g, mla_kv_norm_g, mla_wq_up, mla_wkv_up, mla_qn_g, mla_kn_g,
           w_br_a, w_br_b, w_br_c, w_out):
    cos_t, sin_t, qtab_t = _rope_tables()
    perm = _swap16_index()
    xs = jnp.concatenate([x, ctx], axis=1).reshape(N_ROWS, D)
    cc = jnp.concatenate([c, c_ctx[None, :], jnp.zeros((8 - BATCH - 1, D), F32)], axis=0)
    mods = _modulation(cc, w_mod, b_mod).reshape(DEPTH, 8, 1, 3 * D)

    hk = GLA_H * GLA_DK
    w_tail = _w_in_tail(w_in)
    wup_p = jnp.zeros((DEPTH, 128, 2 * hk), F32)
    wup_p = wup_p.at[:, SMALL_AF:SMALL_AF + GLA_RANK, :hk].set(gla_wa_up_f)
    wup_p = wup_p.at[:, SMALL_AB:SMALL_AB + GLA_RANK, hk:].set(gla_wa_up_b).astype(BF16)
    bup_p = jnp.concatenate([gla_ba_f, gla_ba_b], axis=1)[:, None, :]
    wq4 = mla_wq_up.reshape(DEPTH, MLA_Q_RANK, MLA_H, MLA_QK)
    wq_p = jnp.concatenate([wq4, wq4[:, :, :, MLA_NOPE:][:, :, :, perm]], axis=3)
    wq_p = wq_p.reshape(DEPTH, MLA_Q_RANK, MLA_H * MLA_QPAD).astype(BF16)
    qn_p = jnp.concatenate([mla_qn_g, mla_qn_g[:, MLA_NOPE:][:, perm]], axis=1)[:, None, :]
    kn_p = jnp.pad(mla_kn_g, ((0, 0), (0, MLA_QPAD - MLA_QK)))[:, None, :]
    wkv4 = mla_wkv_up.reshape(DEPTH, MLA_KV_RANK, MLA_H, MLA_NOPE + MLA_DV)
    wk = wkv4[:, :, :, :MLA_NOPE].reshape(DEPTH, MLA_KV_RANK, MLA_H * MLA_NOPE).astype(BF16)
    wvt = wkv4[:, :, :, MLA_NOPE:].reshape(DEPTH, MLA_KV_RANK, MLA_H * MLA_DV).transpose(0, 2, 1).astype(BF16)
    wa, wb, wc, wo = (w.astype(BF16) for w in (w_br_a, w_br_b, w_br_c, w_out))

    for l in range(DEPTH):
        last = l == DEPTH - 1
        u = _inproj(xs, mods[l], norm_g[l][None, :], w_in, l, w_tail)
        q, k, vt = _mla_prep(u, cos_t, sin_t, qtab_t, wq_p[l], wk[l], wvt[l],
                             mla_q_norm_g[l][None, :], mla_kv_norm_g[l][None, :], qn_p[l], kn_p[l])
        o_c = _attention(q, k, vt)
        o_f, o_b = _gla(u, wup_p[l], bup_p[l])
        xs = _merge(u, o_f, o_b, o_c, xs, mods[l], conv_w[l], gla_norm_g[l][None, :],
                    wa, wb, wc, wo, l, last)
    return xs.reshape(BATCH, SEQ, D)
```

```python
import functools
import math

import numpy as np
import jax
import jax.numpy as jnp
from jax import lax
from jax.experimental import pallas as pl
from jax.experimental.pallas import tpu as pltpu

D = 1024
BATCH = 4
SEQ = 4096
DEPTH = 4
GRID_W = 64
CTX = 256
EPS = 1e-6

GLA_H = 4
GLA_DK = 128
GLA_DV = 256
GLA_RANK = 16
GLA_TAU = 16.0
CHUNK = 64

MLA_H = 8
MLA_NOPE = 128
MLA_ROPE = 64
MLA_DV = 128
MLA_QK = MLA_NOPE + MLA_ROPE
MLA_QPAD = 256
VT_ROWS = MLA_DV + 16
_SM_SCALE_LOG2E = (MLA_QK ** -0.5) * math.log2(math.e)
MLA_Q_RANK = 384
MLA_KV_RANK = 128
ROPE_BASE = 10000.0

SB = SEQ + CTX
N_ROWS = BATCH * SB
ROW_TILE = 256
TILES_PER_BATCH = SB // ROW_TILE
X_TILES = SEQ // ROW_TILE
N_TILES = N_ROWS // ROW_TILE
CTX_GROUP = BATCH

IN_PAD = 12288
IN_HEAD_COLS = 7168
COL_AV, COL_AB, COL_AC, COL_AZ, _, COL_BV, COL_BZ, COL_CZ, COL_GA, COL_GB, COL_GC = range(11)
COL_BQ, COL_BK, COL_CQKV = 8, 9, 22
COL_SMALL = 92
SMALL_KR, SMALL_AF, SMALL_AB = 0, 64, 80

IN_TN = 2048
IN_TILES_PER_BATCH = 4
IN_TM = SB // IN_TILES_PER_BATCH
Q_TILE = 1024
Q_UNROLL = 4
KV_TILE = 512
ATTN_LOOKAHEAD = 2

VMEM_LIMIT = 56 * 1024 * 1024

F32 = jnp.float32
BF16 = jnp.bfloat16
ACT = BF16
HALO = 16


def _dot(a, b):
    return jnp.dot(a, b, preferred_element_type=F32)


def _dot_nt(a, b):
    return lax.dot_general(a, b, (((1,), (1,)), ((), ())), preferred_element_type=F32)


def _dot_exact(a, b):
    return jnp.dot(a, b, preferred_element_type=F32, precision=lax.Precision.HIGHEST)


def _sigmoid(x):
    return 1.0 / (1.0 + jnp.exp2(x * -math.log2(math.e)))


def _silu(x):
    return x * _sigmoid(x)


def _tile_group(i):
    tt = i % TILES_PER_BATCH
    return jnp.where(tt >= X_TILES, CTX_GROUP, i // TILES_PER_BATCH)


def _mod_kernel(cc_ref, w_ref, b_ref, o_ref):
    o_ref[0] = _dot_exact(_silu(cc_ref[...]), w_ref[0]) + b_ref[0]


def _modulation(cc, w_mod, b_mod):
    return pl.pallas_call(
        _mod_kernel,
        grid=(DEPTH,),
        in_specs=[
            pl.BlockSpec((8, D), lambda l: (0, 0)),
            pl.BlockSpec((1, D, 3 * D), lambda l: (l, 0, 0)),
            pl.BlockSpec((1, 1, 3 * D), lambda l: (l, 0, 0)),
        ],
        out_specs=pl.BlockSpec((1, 8, 3 * D), lambda l: (l, 0, 0)),
        out_shape=jax.ShapeDtypeStruct((DEPTH, 8, 3 * D), F32),
        compiler_params=pltpu.CompilerParams(
            dimension_semantics=("arbitrary",), vmem_limit_bytes=VMEM_LIMIT),
        name="modulation",
    )(cc, w_mod, b_mod.reshape(DEPTH, 1, 3 * D))


def _inproj_kernel(x_ref, modx_ref, modc_ref, g_ref, w_ref, u_ref, h_ref):
    @pl.when(pl.program_id(1) == 0)
    def _():
        x = x_ref[...]
        y = x * lax.rsqrt(jnp.mean(x * x, axis=-1, keepdims=True) + EPS) * g_ref[...]
        row = (pl.program_id(0) % IN_TILES_PER_BATCH) * IN_TM + lax.broadcasted_iota(jnp.int32, x.shape, 0)
        is_ctx = row >= SEQ
        modx = modx_ref[0]
        modc = modc_ref[0]
        scale = jnp.where(is_ctx, modc[:, D:2 * D], modx[:, D:2 * D])
        shift = jnp.where(is_ctx, modc[:, 0:D], modx[:, 0:D])
        h_ref[...] = (y * (1.0 + scale) + shift).astype(BF16)

    u_ref[...] = _dot(h_ref[...], w_ref[...]).astype(u_ref.dtype)


def _inproj(xs, mod_l, norm_g, w_in_p, layer):
    return pl.pallas_call(
        _inproj_kernel,
        grid=(N_ROWS // IN_TM, IN_PAD // IN_TN),
        in_specs=[
            pl.BlockSpec((IN_TM, D), lambda i, j: (i, 0)),
            pl.BlockSpec((1, 1, 3 * D), lambda i, j: (i // IN_TILES_PER_BATCH, 0, 0)),
            pl.BlockSpec((1, 1, 3 * D), lambda i, j: (CTX_GROUP, 0, 0)),
            pl.BlockSpec((1, D), lambda i, j: (0, 0)),
            pl.BlockSpec((None, D, IN_TN), lambda i, j: (layer, 0, j)),
        ],
        out_specs=pl.BlockSpec((IN_TM, IN_TN), lambda i, j: (i, j)),
        out_shape=jax.ShapeDtypeStruct((N_ROWS, IN_PAD), ACT),
        scratch_shapes=[pltpu.VMEM((IN_TM, D), BF16)],
        compiler_params=pltpu.CompilerParams(
            dimension_semantics=("arbitrary", "arbitrary"), vmem_limit_bytes=VMEM_LIMIT),
        name="inproj",
    )(xs, mod_l, mod_l, norm_g, w_in_p)


def _swap16(r):
    lane = lax.broadcasted_iota(jnp.int32, r.shape, 1)
    even = ((lane // 16) % 2) == 0
    return jnp.where(even, pltpu.roll(r, 112, axis=1), pltpu.roll(r, 16, axis=1))


def _mla_prep_kernel(cqkv_ref, small_ref, cos_ref, sin_ref, qtab_ref, wq_ref, wk_ref, wvt_ref, gq_ref, gkv_ref,
                     qn_ref, kn_ref, q_ref, k_ref, vt_ref):
    cqkv = cqkv_ref[...].astype(F32)
    cq = cqkv[:, :MLA_Q_RANK]
    ckv = cqkv[:, MLA_Q_RANK:]

    cq_n = cq * lax.rsqrt(jnp.mean(cq * cq, axis=-1, keepdims=True) + EPS) * gq_ref[...]
    q_all = _dot(cq_n.astype(BF16), wq_ref[...])
    ckv_n = (ckv * lax.rsqrt(jnp.mean(ckv * ckv, axis=-1, keepdims=True) + EPS) * gkv_ref[...]).astype(BF16)
    kn_all = _dot(ckv_n, wk_ref[...])
    vt_all = _dot_nt(wvt_ref[...], ckv_n)

    small = small_ref[...].astype(F32)
    lane = lax.broadcasted_iota(jnp.int32, small.shape, 1)
    kr = jnp.where(lane < MLA_ROPE, small, 0.0)
    kr_ss = jnp.sum(kr * kr, axis=-1, keepdims=True)
    qn_g = qn_ref[...] * _SM_SCALE_LOG2E
    kn_g = kn_ref[...]
    krg = kr * kn_g[:, MLA_NOPE:MLA_QPAD]
    k_rot = krg * cos_ref[...] + _swap16(krg) * sin_ref[...]
    k_rot2 = k_rot + pltpu.roll(k_rot, MLA_ROPE, axis=1)
    qtab = qtab_ref[...]
    ones_rows = jnp.ones((VT_ROWS - MLA_DV, ROW_TILE), BF16)

    for h in range(MLA_H):
        q0 = q_all[:, h * MLA_QPAD:h * MLA_QPAD + MLA_NOPE]
        q1 = q_all[:, h * MLA_QPAD + MLA_NOPE:(h + 1) * MLA_QPAD]
        ss = jnp.sum(q0 * q0 + 0.5 * (q1 * q1), axis=-1, keepdims=True)
        inv = lax.rsqrt(ss * (1.0 / MLA_QK) + EPS)
        q_ref[h, :, 0:MLA_NOPE] = (q0 * inv * qn_g[:, 0:MLA_NOPE]).astype(BF16)
        q_ref[h, :, MLA_NOPE:MLA_QPAD] = (q1 * inv * qn_g[:, MLA_NOPE:MLA_QPAD] * qtab).astype(BF16)

        kn = kn_all[:, h * MLA_NOPE:(h + 1) * MLA_NOPE]
        inv = lax.rsqrt((jnp.sum(kn * kn, axis=-1, keepdims=True) + kr_ss) * (1.0 / MLA_QK) + EPS)
        k_ref[h, :, 0:MLA_NOPE] = (kn * inv * kn_g[:, 0:MLA_NOPE]).astype(BF16)
        k_ref[h, :, MLA_NOPE:MLA_QPAD] = (k_rot2 * inv).astype(BF16)
        vt_ref[h, 0:MLA_DV, :] = vt_all[h * MLA_DV:(h + 1) * MLA_DV, :].astype(BF16)
        vt_ref[h, MLA_DV:VT_ROWS, :] = ones_rows


def _mla_prep(u, cos_t, sin_t, qtab_t, wq_p, wk, wvt, gq, gkv, qn_p, kn_p):
    const = lambda i: (0, 0)
    table = pl.BlockSpec((ROW_TILE, 128), lambda i: (i % TILES_PER_BATCH, 0))
    return pl.pallas_call(
        _mla_prep_kernel,
        grid=(N_TILES,),
        in_specs=[
            pl.BlockSpec((ROW_TILE, 512), lambda i: (i, COL_CQKV)),
            pl.BlockSpec((ROW_TILE, 128), lambda i: (i, COL_SMALL)),
            table, table, table,
            pl.BlockSpec((MLA_Q_RANK, MLA_H * MLA_QPAD), const),
            pl.BlockSpec((MLA_KV_RANK, MLA_H * MLA_NOPE), const),
            pl.BlockSpec((MLA_H * MLA_DV, MLA_KV_RANK), const),
            pl.BlockSpec((1, MLA_Q_RANK), const),
            pl.BlockSpec((1, MLA_KV_RANK), const),
            pl.BlockSpec((1, MLA_QPAD), const),
            pl.BlockSpec((1, MLA_QPAD), const),
        ],
        out_specs=[
            pl.BlockSpec((MLA_H, ROW_TILE, MLA_QPAD), lambda i: (0, i, 0)),
            pl.BlockSpec((MLA_H, ROW_TILE, MLA_QPAD), lambda i: (0, i, 0)),
            pl.BlockSpec((MLA_H, VT_ROWS, ROW_TILE), lambda i: (0, 0, i)),
        ],
        out_shape=[
            jax.ShapeDtypeStruct((MLA_H, N_ROWS, MLA_QPAD), BF16),
            jax.ShapeDtypeStruct((MLA_H, N_ROWS, MLA_QPAD), BF16),
            jax.ShapeDtypeStruct((MLA_H, VT_ROWS, N_ROWS), BF16),
        ],
        compiler_params=pltpu.CompilerParams(
            dimension_semantics=("arbitrary",), vmem_limit_bytes=VMEM_LIMIT),
        name="mla_prep",
    )(u, u, cos_t, sin_t, qtab_t, wq_p, wk, wvt, gq, gkv, qn_p, kn_p)


def _attn_kernel(q_ref, k_ref, vt_ref, o_ref):
    def softmax_tile(q, kv_bounds):
        m = acc = None
        scores = lambda b: _dot_nt(k_ref[0, b[0]:b[1], :], q)
        ahead = [scores(b) for b in kv_bounds[:ATTN_LOOKAHEAD]]
        for c, (lo, hi) in enumerate(kv_bounds):
            s = ahead.pop(0)
            if c + ATTN_LOOKAHEAD < len(kv_bounds):
                ahead.append(scores(kv_bounds[c + ATTN_LOOKAHEAD]))
            m_c = jnp.max(s, axis=0, keepdims=True)
            m_new = m_c if m is None else jnp.maximum(m, m_c)
            p = jnp.exp2(s - m_new)
            pv = _dot(vt_ref[0, :, lo:hi], p.astype(BF16))
            acc = pv if m is None else jnp.exp2(m - m_new) * acc + pv
            m = m_new
        return (acc[0:MLA_DV] / acc[MLA_DV:MLA_DV + 1]).T.astype(o_ref.dtype)

    n_kv = SEQ // KV_TILE
    lat_bounds = [(c * KV_TILE, (c + 1) * KV_TILE) for c in range(n_kv - 1)] + [((n_kv - 1) * KV_TILE, SB)]

    def q_step(i, carry):
        for t in range(Q_UNROLL):
            r0 = pl.multiple_of((i * Q_UNROLL + t) * Q_TILE, Q_TILE)
            o_ref[pl.ds(r0, Q_TILE), :] = softmax_tile(q_ref[0, pl.ds(r0, Q_TILE), :], lat_bounds)
        return carry

    lax.fori_loop(0, SEQ // (Q_TILE * Q_UNROLL), q_step, 0)
    o_ref[SEQ:SB, :] = softmax_tile(q_ref[0, SEQ:SB, :], [(SEQ, SB)])


def _attention(q, k, vt):
    return pl.pallas_call(
        _attn_kernel,
        grid=(BATCH, MLA_H),
        in_specs=[
            pl.BlockSpec((1, SB, MLA_QPAD), lambda b, h: (h, b, 0)),
            pl.BlockSpec((1, SB, MLA_QPAD), lambda b, h: (h, b, 0)),
            pl.BlockSpec((1, VT_ROWS, SB), lambda b, h: (h, 0, b)),
        ],
        out_specs=pl.BlockSpec((SB, MLA_DV), lambda b, h: (b, h)),
        out_shape=jax.ShapeDtypeStruct((N_ROWS, MLA_H * MLA_DV), ACT),
        compiler_params=pltpu.CompilerParams(
            dimension_semantics=("arbitrary", "arbitrary"), vmem_limit_bytes=VMEM_LIMIT),
        name="mla_attention",
    )(q, k, vt)


def _log2_sigmoid(z):
    z2 = z * math.log2(math.e)
    return jnp.minimum(z2, 0.0) - jnp.log2(1.0 + jnp.exp2(-jnp.abs(z2)))


def _gla_kernel(qf_ref, kf_ref, vf_ref, sf_ref, qb_ref, kb_ref, vb_ref, sb_ref, wup_ref, bup_ref,
                of_ref, ob_ref, st_ref):
    @pl.when(pl.program_id(1) == 0)
    def _():
        st_ref[...] = jnp.zeros_like(st_ref)

    hk = GLA_H * GLA_DK
    n_chunks = ROW_TILE // CHUNK
    row = lax.broadcasted_iota(jnp.int32, (ROW_TILE, ROW_TILE), 0)
    col = lax.broadcasted_iota(jnp.int32, (ROW_TILE, ROW_TILE), 1)
    same_chunk = (row // CHUNK) == (col // CHUNK)
    q_scale = GLA_DK ** -0.5

    dirs = ((qf_ref, kf_ref, vf_ref, sf_ref, of_ref), (qb_ref, kb_ref, vb_ref, sb_ref, ob_ref))
    heads = [(d, h) for d in range(2) for h in range(GLA_H)]
    ksl = lambda h: slice(h * GLA_DK, (h + 1) * GLA_DK)
    vsl = lambda h: slice(h * GLA_DV, (h + 1) * GLA_DV)
    csl = lambda c: slice(c * CHUNK, (c + 1) * CHUNK)
    keep = [jnp.logical_and(same_chunk, row >= col), jnp.logical_and(same_chunk, row <= col)]
    scan = [list(range(n_chunks)), list(range(n_chunks - 1, -1, -1))]

    z = [_dot(dirs[d][3][...], wup_ref[:, d * hk:(d + 1) * hk]) + bup_ref[:, d * hk:(d + 1) * hk] for d in range(2)]
    cs = []
    for d in range(2):
        la = _log2_sigmoid(z[d]) * (1.0 / GLA_TAU)
        la_hi = la.astype(BF16)
        la_lo = (la - la_hi.astype(F32)).astype(BF16)
        tri = jnp.where(keep[d], 1.0, 0.0).astype(BF16)
        cs.append(_dot(tri, jnp.concatenate([la_hi, la_lo], axis=1)))

    q_dec, k_inv, k_end, g = [], [], [], []
    for d in range(2):
        q_ref, k_ref = dirs[d][0], dirs[d][1]
        b = cs[d][:, :hk] + cs[d][:, hk:]
        edge = (lambda c: c * CHUNK + CHUNK - 1) if d == 0 else (lambda c: c * CHUNK)
        tots = [b[edge(c):edge(c) + 1] for c in range(n_chunks)]
        tot = jnp.concatenate([jnp.broadcast_to(t, (CHUNK, hk)) for t in tots], axis=0)
        kf = k_ref[...].astype(F32)
        q_dec.append(((q_ref[...].astype(F32) * q_scale) * jnp.exp2(b)).astype(BF16))
        k_inv.append((kf * jnp.exp2(-b)).astype(BF16))
        k_end.append(kf * jnp.exp2(tot - b))
        g.append([jnp.exp2(t) for t in tots])

    att = {(d, h): _dot_nt(q_dec[d][:, ksl(h)], k_inv[d][:, ksl(h)]) for (d, h) in heads}
    upd = {(d, h, c): _dot(k_end[d][csl(c), ksl(h)].T.astype(BF16), dirs[d][2][csl(c), vsl(h)])
           for (d, h) in heads for c in range(n_chunks)}
    o_intra = {(d, h): _dot(jnp.where(keep[d], att[d, h], 0.0).astype(BF16), dirs[d][2][:, vsl(h)])
               for (d, h) in heads}

    states = {}
    for (d, h) in heads:
        st = st_ref[d, h]
        for c in scan[d]:
            states[d, h, c] = st.astype(BF16)
            g_col = jnp.broadcast_to(g[d][c][:, ksl(h)], (GLA_DK, GLA_DK)).T
            st = st * jnp.concatenate([g_col] * (GLA_DV // GLA_DK), axis=1) + upd[d, h, c]
        st_ref[d, h] = st

    for (d, h) in heads:
        o_ref = dirs[d][4]
        for c in range(n_chunks):
            o_inter = _dot(q_dec[d][csl(c), ksl(h)], states[d, h, c])
            o_ref[csl(c), vsl(h)] = (o_intra[d, h][csl(c)] + o_inter).astype(o_ref.dtype)


def _gla(u, wup_p, bup_p):
    def fwd_tile(b, s):
        return b * TILES_PER_BATCH + jnp.where(s == 0, X_TILES, s - 1)

    def bwd_tile(b, s):
        return b * TILES_PER_BATCH + jnp.where(s == 0, X_TILES, X_TILES - s)

    def specs(tile):
        return [
            pl.BlockSpec((ROW_TILE, 512), lambda b, s: (tile(b, s), COL_BQ)),
            pl.BlockSpec((ROW_TILE, 512), lambda b, s: (tile(b, s), COL_BK)),
            pl.BlockSpec((ROW_TILE, 1024), lambda b, s: (tile(b, s), COL_BV)),
            pl.BlockSpec((ROW_TILE, 128), lambda b, s: (tile(b, s), COL_SMALL)),
        ]

    return pl.pallas_call(
        _gla_kernel,
        grid=(BATCH, TILES_PER_BATCH),
        in_specs=specs(fwd_tile) + specs(bwd_tile) + [
            pl.BlockSpec((128, 2 * GLA_H * GLA_DK), lambda b, s: (0, 0)),
            pl.BlockSpec((1, 2 * GLA_H * GLA_DK), lambda b, s: (0, 0)),
        ],
        out_specs=[
            pl.BlockSpec((ROW_TILE, GLA_H * GLA_DV), lambda b, s: (fwd_tile(b, s), 0)),
            pl.BlockSpec((ROW_TILE, GLA_H * GLA_DV), lambda b, s: (bwd_tile(b, s), 0)),
        ],
        out_shape=[jax.ShapeDtypeStruct((N_ROWS, GLA_H * GLA_DV), ACT)] * 2,
        scratch_shapes=[pltpu.VMEM((2, GLA_H, GLA_DK, GLA_DV), F32)],
        compiler_params=pltpu.CompilerParams(
            dimension_semantics=("arbitrary", "arbitrary"), vmem_limit_bytes=VMEM_LIMIT),
        name="gla_bidir",
    )(u, u, u, u, u, u, u, u, wup_p, bup_p)


def _merge_kernel(av_ref, ab_ref, ac_ref, az_ref, bz_ref, cz_ref, ga_ref, gb_ref, gc_ref,
                  avp_ref, acp_ref, avn_ref, acn_ref, of_ref, ob_ref, oc_ref, x_ref, mod_ref,
                  cw_ref, gng_ref, wa_ref, wb_ref, wc_ref, wo_ref, out_ref, *, tiles_per_batch):
    tt = pl.program_id(0) % tiles_per_batch
    prev_ok = jnp.logical_and(tt != 0, tt < X_TILES).astype(F32)
    next_ok = (tt < X_TILES - 1).astype(F32)

    f32 = lambda ref: ref[...].astype(F32)
    p = f32(ac_ref) * f32(av_ref)
    p_prev = f32(acp_ref)[HALO - 1:HALO, :] * f32(avp_ref)[HALO - 1:HALO, :] * prev_ok
    p_next = f32(acn_ref)[0:1, :] * f32(avn_ref)[0:1, :] * next_ok
    rows = lax.broadcasted_iota(jnp.int32, p.shape, 0)
    p_up = jnp.where(rows == 0, p_prev, pltpu.roll(p, 1, axis=0))
    p_dn = jnp.where(rows == ROW_TILE - 1, p_next, pltpu.roll(p, ROW_TILE - 1, axis=0))
    cw = cw_ref[...]
    conv = p_up * cw[0:1] + p * cw[1:2] + p_dn * cw[2:3]
    y_a = f32(ab_ref) * conv * _silu(f32(az_ref))
    m = _sigmoid(f32(ga_ref)) * _dot(y_a.astype(BF16), wa_ref[...])

    o = f32(of_ref) + f32(ob_ref)
    gng = gng_ref[...]
    y_b = []
    for h in range(GLA_H):
        sl = slice(h * GLA_DV, (h + 1) * GLA_DV)
        oh = o[:, sl]
        y_b.append(oh * lax.rsqrt(jnp.mean(oh * oh, axis=-1, keepdims=True) + EPS) * gng[:, sl])
    y_b = jnp.concatenate(y_b, axis=-1) * _silu(f32(bz_ref))
    m = m + _sigmoid(f32(gb_ref)) * _dot(y_b.astype(BF16), wb_ref[...])

    y_c = f32(oc_ref) * _silu(f32(cz_ref))
    m = m + _sigmoid(f32(gc_ref)) * _dot(y_c.astype(BF16), wc_ref[...])

    gate = mod_ref[0][:, 2 * D:3 * D]
    out_ref[...] = x_ref[...] + gate * _dot(m.astype(BF16), wo_ref[...])


def _merge(u, o_f, o_b, o_c, xs, mod_l, conv_w, gla_norm_g, wa, wb, wc, wo, layer, last):
    halo_blocks = ROW_TILE // HALO
    n_halo = N_ROWS // HALO
    if last:
        grid = (BATCH * X_TILES,)
        tile = lambda i: (i // X_TILES) * TILES_PER_BATCH + i % X_TILES
        out_rows = BATCH * SEQ
        out_map = lambda i: (i, 0)
    else:
        grid = (N_TILES,)
        tile = lambda i: i
        out_rows = N_ROWS
        out_map = lambda i: (i, 0)

    def col(c):
        return pl.BlockSpec((ROW_TILE, D), lambda i: (tile(i), c))

    def prev_rows(c):
        return pl.BlockSpec((HALO, D), lambda i: (jnp.maximum(tile(i) * halo_blocks - 1, 0), c))

    def next_rows(c):
        return pl.BlockSpec((HALO, D), lambda i: (jnp.minimum((tile(i) + 1) * halo_blocks, n_halo - 1), c))

    row_block = pl.BlockSpec((ROW_TILE, D), lambda i: (tile(i), 0))
    const = lambda i: (0, 0)
    weight = pl.BlockSpec((None, D, D), lambda i: (layer, 0, 0))

    return pl.pallas_call(
        functools.partial(_merge_kernel, tiles_per_batch=X_TILES if last else TILES_PER_BATCH),
        grid=grid,
        in_specs=[col(COL_AV), col(COL_AB), col(COL_AC), col(COL_AZ), col(COL_BZ), col(COL_CZ),
                  col(COL_GA), col(COL_GB), col(COL_GC),
                  prev_rows(COL_AV), prev_rows(COL_AC), next_rows(COL_AV), next_rows(COL_AC),
                  row_block, row_block, row_block, row_block,
                  pl.BlockSpec((1, 1, 3 * D), lambda i: (_tile_group(tile(i)), 0, 0)),
                  pl.BlockSpec((3, D), const), pl.BlockSpec((1, D), const),
                  weight, weight, weight, weight],
        out_specs=pl.BlockSpec((ROW_TILE, D), out_map),
        out_shape=jax.ShapeDtypeStruct((out_rows, D), F32),
        compiler_params=pltpu.CompilerParams(
            dimension_semantics=("arbitrary",), vmem_limit_bytes=VMEM_LIMIT),
        name="merge_last" if last else "merge",
    )(u, u, u, u, u, u, u, u, u, u, u, u, u, o_f, o_b, o_c, xs, mod_l, conv_w, gla_norm_g,
      wa, wb, wc, wo)


def _rope_tables():
    t = np.arange(SEQ)
    row = (t // GRID_W).astype(np.float32)
    colp = (t % GRID_W).astype(np.float32)
    n_freq = MLA_ROPE // 4
    freqs = (np.float32(ROPE_BASE) ** (-np.arange(n_freq, dtype=np.float32) / np.float32(n_freq))).astype(np.float32)
    ang_r = row[:, None] * freqs[None, :]
    ang_c = colp[:, None] * freqs[None, :]
    ang = np.concatenate([ang_r, ang_r, ang_c, ang_c], axis=-1).astype(np.float32)
    cos = np.ones((SB, 128), np.float32)
    sin = np.zeros((SB, 128), np.float32)
    cos[:SEQ, :MLA_ROPE] = np.cos(ang)
    sign = np.where((np.arange(MLA_ROPE) // n_freq) % 2 == 0, -1.0, 1.0).astype(np.float32)
    sin[:SEQ, :MLA_ROPE] = np.sin(ang) * sign[None, :]
    qtab = np.concatenate([cos[:, :MLA_ROPE], sin[:, :MLA_ROPE]], axis=1)
    return jnp.asarray(cos), jnp.asarray(sin), jnp.asarray(qtab)


def _swap16_index():
    l = np.arange(MLA_ROPE)
    return np.where((l // 16) % 2 == 0, l + 16, l - 16)


def _permute_w_in(w_in):
    seg = lambda a, n: w_in[:, :, a:a + n]
    parts = [
        seg(0, IN_HEAD_COLS),
        seg(7776, 1024),
        seg(8800, 3072),
        seg(7200, 384), seg(7584, 128),
        seg(7712, 64), seg(7168, 16), seg(7184, 16),
    ]
    used = sum(p.shape[2] for p in parts)
    parts.append(jnp.zeros((DEPTH, D, IN_PAD - used), w_in.dtype))
    return jnp.concatenate(parts, axis=2).astype(BF16)


def kernel(x, c, ctx, c_ctx, w_mod, b_mod, norm_g, w_in, conv_w, gla_wa_up_f, gla_ba_f, gla_wa_up_b,
           gla_ba_b, gla_norm_g, mla_q_norm_g, mla_kv_norm_g, mla_wq_up, mla_wkv_up, mla_qn_g, mla_kn_g,
           w_br_a, w_br_b, w_br_c, w_out):
    cos_t, sin_t, qtab_t = _rope_tables()
    perm = _swap16_index()
    xs = jnp.concatenate([x, ctx], axis=1).reshape(N_ROWS, D)
    cc = jnp.concatenate([c, c_ctx[None, :], jnp.zeros((8 - BATCH - 1, D), F32)], axis=0)
    mods = _modulation(cc, w_mod, b_mod).reshape(DEPTH, 8, 1, 3 * D)

    hk = GLA_H * GLA_DK
    w_in_p = _permute_w_in(w_in)
    wup_p = jnp.zeros((DEPTH, 128, 2 * hk), F32)
    wup_p = wup_p.at[:, SMALL_AF:SMALL_AF + GLA_RANK, :hk].set(gla_wa_up_f)
    wup_p = wup_p.at[:, SMALL_AB:SMALL_AB + GLA_RANK, hk:].set(gla_wa_up_b).astype(BF16)
    bup_p = jnp.concatenate([gla_ba_f, gla_ba_b], axis=1)[:, None, :]
    wq4 = mla_wq_up.reshape(DEPTH, MLA_Q_RANK, MLA_H, MLA_QK)
    wq_p = jnp.concatenate([wq4, wq4[:, :, :, MLA_NOPE:][:, :, :, perm]], axis=3)
    wq_p = wq_p.reshape(DEPTH, MLA_Q_RANK, MLA_H * MLA_QPAD).astype(BF16)
    qn_p = jnp.concatenate([mla_qn_g, mla_qn_g[:, MLA_NOPE:][:, perm]], axis=1)[:, None, :]
    kn_p = jnp.pad(mla_kn_g, ((0, 0), (0, MLA_QPAD - MLA_QK)))[:, None, :]
    wkv4 = mla_wkv_up.reshape(DEPTH, MLA_KV_RANK, MLA_H, MLA_NOPE + MLA_DV)
    wk = wkv4[:, :, :, :MLA_NOPE].reshape(DEPTH, MLA_KV_RANK, MLA_H * MLA_NOPE).astype(BF16)
    wvt = wkv4[:, :, :, MLA_NOPE:].reshape(DEPTH, MLA_KV_RANK, MLA_H * MLA_DV).transpose(0, 2, 1).astype(BF16)
    wa, wb, wc, wo = (w.astype(BF16) for w in (w_br_a, w_br_b, w_br_c, w_out))

    for l in range(DEPTH):
        last = l == DEPTH - 1
        u = _inproj(xs, mods[l], norm_g[l][None, :], w_in_p, l)
        q, k, vt = _mla_prep(u, cos_t, sin_t, qtab_t, wq_p[l], wk[l], wvt[l],
                             mla_q_norm_g[l][None, :], mla_kv_norm_g[l][None, :], qn_p[l], kn_p[l])
        o_c = _attention(q, k, vt)
        o_f, o_b = _gla(u, wup_p[l], bup_p[l])
        xs = _merge(u, o_f, o_b, o_c, xs, mods[l], conv_w[l], gla_norm_g[l][None, :],
                    wa, wb, wc, wo, l, last)
    return xs.reshape(BATCH, SEQ, D)
```

```python
import functools
import math

import numpy as np
import jax
import jax.numpy as jnp
from jax import lax
from jax.experimental import pallas as pl
from jax.experimental.pallas import tpu as pltpu

D = 1024
BATCH = 4
SEQ = 4096
DEPTH = 4
GRID_W = 64
CTX = 256
EPS = 1e-6

GLA_H = 4
GLA_DK = 128
GLA_DV = 256
GLA_RANK = 16
GLA_TAU = 16.0
CHUNK = 64

MLA_H = 8
MLA_NOPE = 128
MLA_ROPE = 64
MLA_DV = 128
MLA_QK = MLA_NOPE + MLA_ROPE
MLA_QPAD = 256
VT_ROWS = MLA_DV + 16
_SM_SCALE_LOG2E = (MLA_QK ** -0.5) * math.log2(math.e)
MLA_Q_RANK = 384
MLA_KV_RANK = 128
ROPE_BASE = 10000.0

SB = SEQ + CTX
N_ROWS = BATCH * SB
ROW_TILE = 256
TILES_PER_BATCH = SB // ROW_TILE
X_TILES = SEQ // ROW_TILE
N_TILES = N_ROWS // ROW_TILE
CTX_GROUP = BATCH

IN_PAD = 12288
IN_HEAD_COLS = 7168
COL_AV, COL_AB, COL_AC, COL_AZ, _, COL_BV, COL_BZ, COL_CZ, COL_GA, COL_GB, COL_GC = range(11)
COL_BQ, COL_BK, COL_CQKV = 8, 9, 22
COL_SMALL = 92
SMALL_KR, SMALL_AF, SMALL_AB = 0, 64, 80

IN_TN = 2048
IN_TILES_PER_BATCH = 4
IN_TM = SB // IN_TILES_PER_BATCH
Q_TILE = 1024
Q_UNROLL = 4
KV_TILE = 512
ATTN_LOOKAHEAD = 2

VMEM_LIMIT = 56 * 1024 * 1024

F32 = jnp.float32
BF16 = jnp.bfloat16
ACT = BF16
HALO = 16


def _dot(a, b):
    return jnp.dot(a, b, preferred_element_type=F32)


def _dot_nt(a, b):
    return lax.dot_general(a, b, (((1,), (1,)), ((), ())), preferred_element_type=F32)


def _dot_exact(a, b):
    return jnp.dot(a, b, preferred_element_type=F32, precision=lax.Precision.HIGHEST)


def _sigmoid(x):
    return 1.0 / (1.0 + jnp.exp2(x * -math.log2(math.e)))


def _silu(x):
    return x * _sigmoid(x)


def _tile_group(i):
    tt = i % TILES_PER_BATCH
    return jnp.where(tt >= X_TILES, CTX_GROUP, i // TILES_PER_BATCH)


def _mod_kernel(cc_ref, w_ref, b_ref, o_ref):
    o_ref[0] = _dot_exact(_silu(cc_ref[...]), w_ref[0]) + b_ref[0]


def _modulation(cc, w_mod, b_mod):
    return pl.pallas_call(
        _mod_kernel,
        grid=(DEPTH,),
        in_specs=[
            pl.BlockSpec((8, D), lambda l: (0, 0)),
            pl.BlockSpec((1, D, 3 * D), lambda l: (l, 0, 0)),
            pl.BlockSpec((1, 1, 3 * D), lambda l: (l, 0, 0)),
        ],
        out_specs=pl.BlockSpec((1, 8, 3 * D), lambda l: (l, 0, 0)),
        out_shape=jax.ShapeDtypeStruct((DEPTH, 8, 3 * D), F32),
        compiler_params=pltpu.CompilerParams(
            dimension_semantics=("arbitrary",), vmem_limit_bytes=VMEM_LIMIT),
        name="modulation",
    )(cc, w_mod, b_mod.reshape(DEPTH, 1, 3 * D))


def _inproj_kernel(x_ref, modx_ref, modc_ref, g_ref, w_ref, u_ref, h_ref):
    @pl.when(pl.program_id(1) == 0)
    def _():
        x = x_ref[...]
        y = x * lax.rsqrt(jnp.mean(x * x, axis=-1, keepdims=True) + EPS) * g_ref[...]
        row = (pl.program_id(0) % IN_TILES_PER_BATCH) * IN_TM + lax.broadcasted_iota(jnp.int32, x.shape, 0)
        is_ctx = row >= SEQ
        modx = modx_ref[0]
        modc = modc_ref[0]
        scale = jnp.where(is_ctx, modc[:, D:2 * D], modx[:, D:2 * D])
        shift = jnp.where(is_ctx, modc[:, 0:D], modx[:, 0:D])
        h_ref[...] = (y * (1.0 + scale) + shift).astype(BF16)

    u_ref[...] = _dot(h_ref[...], w_ref[...]).astype(u_ref.dtype)


def _inproj(xs, mod_l, norm_g, w_in_p, layer):
    return pl.pallas_call(
        _inproj_kernel,
        grid=(N_ROWS // IN_TM, IN_PAD // IN_TN),
        in_specs=[
            pl.BlockSpec((IN_TM, D), lambda i, j: (i, 0)),
            pl.BlockSpec((1, 1, 3 * D), lambda i, j: (i // IN_TILES_PER_BATCH, 0, 0)),
            pl.BlockSpec((1, 1, 3 * D), lambda i, j: (CTX_GROUP, 0, 0)),
            pl.BlockSpec((1, D), lambda i, j: (0, 0)),
            pl.BlockSpec((None, D, IN_TN), lambda i, j: (layer, 0, j)),
        ],
        out_specs=pl.BlockSpec((IN_TM, IN_TN), lambda i, j: (i, j)),
        out_shape=jax.ShapeDtypeStruct((N_ROWS, IN_PAD), ACT),
        scratch_shapes=[pltpu.VMEM((IN_TM, D), BF16)],
        compiler_params=pltpu.CompilerParams(
            dimension_semantics=("arbitrary", "arbitrary"), vmem_limit_bytes=VMEM_LIMIT),
        name="inproj",
    )(xs, mod_l, mod_l, norm_g, w_in_p)


def _swap16(r):
    lane = lax.broadcasted_iota(jnp.int32, r.shape, 1)
    even = ((lane // 16) % 2) == 0
    return jnp.where(even, pltpu.roll(r, 112, axis=1), pltpu.roll(r, 16, axis=1))


def _mla_prep_kernel(cqkv_ref, small_ref, cos_ref, sin_ref, qtab_ref, wq_ref, wk_ref, wvt_ref, gq_ref, gkv_ref,
                     qn_ref, kn_ref, q_ref, k_ref, vt_ref):
    cqkv = cqkv_ref[...].astype(F32)
    cq = cqkv[:, :MLA_Q_RANK]
    ckv = cqkv[:, MLA_Q_RANK:]

    cq_n = cq * lax.rsqrt(jnp.mean(cq * cq, axis=-1, keepdims=True) + EPS) * gq_ref[...]
    q_all = _dot(cq_n.astype(BF16), wq_ref[...])
    ckv_n = (ckv * lax.rsqrt(jnp.mean(ckv * ckv, axis=-1, keepdims=True) + EPS) * gkv_ref[...]).astype(BF16)
    kn_all = _dot(ckv_n, wk_ref[...])
    vt_all = _dot_nt(wvt_ref[...], ckv_n)

    small = small_ref[...].astype(F32)
    lane = lax.broadcasted_iota(jnp.int32, small.shape, 1)
    kr = jnp.where(lane < MLA_ROPE, small, 0.0)
    kr_ss = jnp.sum(kr * kr, axis=-1, keepdims=True)
    qn_g = qn_ref[...] * _SM_SCALE_LOG2E
    kn_g = kn_ref[...]
    krg = kr * kn_g[:, MLA_NOPE:MLA_QPAD]
    k_rot = krg * cos_ref[...] + _swap16(krg) * sin_ref[...]
    k_rot2 = k_rot + pltpu.roll(k_rot, MLA_ROPE, axis=1)
    qtab = qtab_ref[...]
    ones_rows = jnp.ones((VT_ROWS - MLA_DV, ROW_TILE), BF16)

    for h in range(MLA_H):
        q0 = q_all[:, h * MLA_QPAD:h * MLA_QPAD + MLA_NOPE]
        q1 = q_all[:, h * MLA_QPAD + MLA_NOPE:(h + 1) * MLA_QPAD]
        ss = jnp.sum(q0 * q0 + 0.5 * (q1 * q1), axis=-1, keepdims=True)
        inv = lax.rsqrt(ss * (1.0 / MLA_QK) + EPS)
        q_ref[h, :, 0:MLA_NOPE] = (q0 * inv * qn_g[:, 0:MLA_NOPE]).astype(BF16)
        q_ref[h, :, MLA_NOPE:MLA_QPAD] = (q1 * inv * qn_g[:, MLA_NOPE:MLA_QPAD] * qtab).astype(BF16)

        kn = kn_all[:, h * MLA_NOPE:(h + 1) * MLA_NOPE]
        inv = lax.rsqrt((jnp.sum(kn * kn, axis=-1, keepdims=True) + kr_ss) * (1.0 / MLA_QK) + EPS)
        k_ref[h, :, 0:MLA_NOPE] = (kn * inv * kn_g[:, 0:MLA_NOPE]).astype(BF16)
        k_ref[h, :, MLA_NOPE:MLA_QPAD] = (k_rot2 * inv).astype(BF16)
        vt_ref[h, 0:MLA_DV, :] = vt_all[h * MLA_DV:(h + 1) * MLA_DV, :].astype(BF16)
        vt_ref[h, MLA_DV:VT_ROWS, :] = ones_rows


def _mla_prep(u, cos_t, sin_t, qtab_t, wq_p, wk, wvt, gq, gkv, qn_p, kn_p):
    const = lambda i: (0, 0)
    table = pl.BlockSpec((ROW_TILE, 128), lambda i: (i % TILES_PER_BATCH, 0))
    return pl.pallas_call(
        _mla_prep_kernel,
        grid=(N_TILES,),
        in_specs=[
            pl.BlockSpec((ROW_TILE, 512), lambda i: (i, COL_CQKV)),
            pl.BlockSpec((ROW_TILE, 128), lambda i: (i, COL_SMALL)),
            table, table, table,
            pl.BlockSpec((MLA_Q_RANK, MLA_H * MLA_QPAD), const),
            pl.BlockSpec((MLA_KV_RANK, MLA_H * MLA_NOPE), const),
            pl.BlockSpec((MLA_H * MLA_DV, MLA_KV_RANK), const),
            pl.BlockSpec((1, MLA_Q_RANK), const),
            pl.BlockSpec((1, MLA_KV_RANK), const),
            pl.BlockSpec((1, MLA_QPAD), const),
            pl.BlockSpec((1, MLA_QPAD), const),
        ],
        out_specs=[
            pl.BlockSpec((MLA_H, ROW_TILE, MLA_QPAD), lambda i: (0, i, 0)),
            pl.BlockSpec((MLA_H, ROW_TILE, MLA_QPAD), lambda i: (0, i, 0)),
            pl.BlockSpec((MLA_H, VT_ROWS, ROW_TILE), lambda i: (0, 0, i)),
        ],
        out_shape=[
            jax.ShapeDtypeStruct((MLA_H, N_ROWS, MLA_QPAD), BF16),
            jax.ShapeDtypeStruct((MLA_H, N_ROWS, MLA_QPAD), BF16),
            jax.ShapeDtypeStruct((MLA_H, VT_ROWS, N_ROWS), BF16),
        ],
        compiler_params=pltpu.CompilerParams(
            dimension_semantics=("arbitrary",), vmem_limit_bytes=VMEM_LIMIT),
        name="mla_prep",
    )(u, u, cos_t, sin_t, qtab_t, wq_p, wk, wvt, gq, gkv, qn_p, kn_p)


def _attn_kernel(q_ref, k_ref, vt_ref, o_ref):
    def softmax_tile(q, kv_bounds):
        m = acc = None
        scores = lambda b: _dot_nt(k_ref[0, b[0]:b[1], :], q)
        ahead = [scores(b) for b in kv_bounds[:ATTN_LOOKAHEAD]]
        for c, (lo, hi) in enumerate(kv_bounds):
            s = ahead.pop(0)
            if c + ATTN_LOOKAHEAD < len(kv_bounds):
                ahead.append(scores(kv_bounds[c + ATTN_LOOKAHEAD]))
            m_c = jnp.max(s, axis=0, keepdims=True)
            m_new = m_c if m is None else jnp.maximum(m, m_c)
            p = jnp.exp2(s - m_new)
            pv = _dot(vt_ref[0, :, lo:hi], p.astype(BF16))
            acc = pv if m is None else jnp.exp2(m - m_new) * acc + pv
            m = m_new
        return (acc[0:MLA_DV] / acc[MLA_DV:MLA_DV + 1]).T.astype(o_ref.dtype)

    n_kv = SEQ // KV_TILE
    lat_bounds = [(c * KV_TILE, (c + 1) * KV_TILE) for c in range(n_kv - 1)] + [((n_kv - 1) * KV_TILE, SB)]

    def q_step(i, carry):
        for t in range(Q_UNROLL):
            r0 = pl.multiple_of((i * Q_UNROLL + t) * Q_TILE, Q_TILE)
            o_ref[pl.ds(r0, Q_TILE), :] = softmax_tile(q_ref[0, pl.ds(r0, Q_TILE), :], lat_bounds)
        return carry

    lax.fori_loop(0, SEQ // (Q_TILE * Q_UNROLL), q_step, 0)
    o_ref[SEQ:SB, :] = softmax_tile(q_ref[0, SEQ:SB, :], [(SEQ, SB)])


def _attention(q, k, vt):
    return pl.pallas_call(
        _attn_kernel,
        grid=(BATCH, MLA_H),
        in_specs=[
            pl.BlockSpec((1, SB, MLA_QPAD), lambda b, h: (h, b, 0)),
            pl.BlockSpec((1, SB, MLA_QPAD), lambda b, h: (h, b, 0)),
            pl.BlockSpec((1, VT_ROWS, SB), lambda b, h: (h, 0, b)),
        ],
        out_specs=pl.BlockSpec((SB, MLA_DV), lambda b, h: (b, h)),
        out_shape=jax.ShapeDtypeStruct((N_ROWS, MLA_H * MLA_DV), ACT),
        compiler_params=pltpu.CompilerParams(
            dimension_semantics=("arbitrary", "arbitrary"), vmem_limit_bytes=VMEM_LIMIT),
        name="mla_attention",
    )(q, k, vt)


def _log2_sigmoid(z):
    z2 = z * math.log2(math.e)
    return jnp.minimum(z2, 0.0) - jnp.log2(1.0 + jnp.exp2(-jnp.abs(z2)))


def _gla_kernel(qf_ref, kf_ref, vf_ref, sf_ref, qb_ref, kb_ref, vb_ref, sb_ref, wup_ref, bup_ref,
                of_ref, ob_ref, st_ref):
    @pl.when(pl.program_id(1) == 0)
    def _():
        st_ref[...] = jnp.zeros_like(st_ref)

    hk = GLA_H * GLA_DK
    n_chunks = ROW_TILE // CHUNK
    row = lax.broadcasted_iota(jnp.int32, (ROW_TILE, ROW_TILE), 0)
    col = lax.broadcasted_iota(jnp.int32, (ROW_TILE, ROW_TILE), 1)
    same_chunk = (row // CHUNK) == (col // CHUNK)
    q_scale = GLA_DK ** -0.5

    dirs = ((qf_ref, kf_ref, vf_ref, sf_ref, of_ref), (qb_ref, kb_ref, vb_ref, sb_ref, ob_ref))
    heads = [(d, h) for d in range(2) for h in range(GLA_H)]
    ksl = lambda h: slice(h * GLA_DK, (h + 1) * GLA_DK)
    vsl = lambda h: slice(h * GLA_DV, (h + 1) * GLA_DV)
    csl = lambda c: slice(c * CHUNK, (c + 1) * CHUNK)
    keep = [jnp.logical_and(same_chunk, row >= col), jnp.logical_and(same_chunk, row <= col)]
    scan = [list(range(n_chunks)), list(range(n_chunks - 1, -1, -1))]

    z = [_dot(dirs[d][3][...], wup_ref[:, d * hk:(d + 1) * hk]) + bup_ref[:, d * hk:(d + 1) * hk] for d in range(2)]
    cs = []
    for d in range(2):
        la = _log2_sigmoid(z[d]) * (1.0 / GLA_TAU)
        la_hi = la.astype(BF16)
        la_lo = (la - la_hi.astype(F32)).astype(BF16)
        tri = jnp.where(keep[d], 1.0, 0.0).astype(BF16)
        cs.append(_dot(tri, jnp.concatenate([la_hi, la_lo], axis=1)))

    q_dec, k_inv, k_end, g = [], [], [], []
    for d in range(2):
        q_ref, k_ref = dirs[d][0], dirs[d][1]
        b = cs[d][:, :hk] + cs[d][:, hk:]
        edge = (lambda c: c * CHUNK + CHUNK - 1) if d == 0 else (lambda c: c * CHUNK)
        tots = [b[edge(c):edge(c) + 1] for c in range(n_chunks)]
        tot = jnp.concatenate([jnp.broadcast_to(t, (CHUNK, hk)) for t in tots], axis=0)
        kf = k_ref[...].astype(F32)
        q_dec.append(((q_ref[...].astype(F32) * q_scale) * jnp.exp2(b)).astype(BF16))
        k_inv.append((kf * jnp.exp2(-b)).astype(BF16))
        k_end.append(kf * jnp.exp2(tot - b))
        g.append([jnp.exp2(t) for t in tots])

    att = {(d, h): _dot_nt(q_dec[d][:, ksl(h)], k_inv[d][:, ksl(h)]) for (d, h) in heads}
    upd = {(d, h, c): _dot(k_end[d][csl(c), ksl(h)].T.astype(BF16), dirs[d][2][csl(c), vsl(h)])
           for (d, h) in heads for c in range(n_chunks)}
    o_intra = {(d, h): _dot(jnp.where(keep[d], att[d, h], 0.0).astype(BF16), dirs[d][2][:, vsl(h)])
               for (d, h) in heads}

    states = {}
    for (d, h) in heads:
        st = st_ref[d, h]
        for c in scan[d]:
            states[d, h, c] = st.astype(BF16)
            g_col = jnp.broadcast_to(g[d][c][:, ksl(h)], (GLA_DK, GLA_DK)).T
            st = st * jnp.concatenate([g_col] * (GLA_DV // GLA_DK), axis=1) + upd[d, h, c]
        st_ref[d, h] = st

    for (d, h) in heads:
        o_ref = dirs[d][4]
        for c in range(n_chunks):
            o_inter = _dot(q_dec[d][csl(c), ksl(h)], states[d, h, c])
            o_ref[csl(c), vsl(h)] = (o_intra[d, h][csl(c)] + o_inter).astype(o_ref.dtype)


def _gla(u, wup_p, bup_p):
    def fwd_tile(b, s):
        return b * TILES_PER_BATCH + jnp.where(s == 0, X_TILES, s - 1)

    def bwd_tile(b, s):
        return b * TILES_PER_BATCH + jnp.where(s == 0, X_TILES, X_TILES - s)

    def specs(tile):
        return [
            pl.BlockSpec((ROW_TILE, 512), lambda b, s: (tile(b, s), COL_BQ)),
            pl.BlockSpec((ROW_TILE, 512), lambda b, s: (tile(b, s), COL_BK)),
            pl.BlockSpec((ROW_TILE, 1024), lambda b, s: (tile(b, s), COL_BV)),
            pl.BlockSpec((ROW_TILE, 128), lambda b, s: (tile(b, s), COL_SMALL)),
        ]

    return pl.pallas_call(
        _gla_kernel,
        grid=(BATCH, TILES_PER_BATCH),
        in_specs=specs(fwd_tile) + specs(bwd_tile) + [
            pl.BlockSpec((128, 2 * GLA_H * GLA_DK), lambda b, s: (0, 0)),
            pl.BlockSpec((1, 2 * GLA_H * GLA_DK), lambda b, s: (0, 0)),
        ],
        out_specs=[
            pl.BlockSpec((ROW_TILE, GLA_H * GLA_DV), lambda b, s: (fwd_tile(b, s), 0)),
            pl.BlockSpec((ROW_TILE, GLA_H * GLA_DV), lambda b, s: (bwd_tile(b, s), 0)),
        ],
        out_shape=[jax.ShapeDtypeStruct((N_ROWS, GLA_H * GLA_DV), ACT)] * 2,
        scratch_shapes=[pltpu.VMEM((2, GLA_H, GLA_DK, GLA_DV), F32)],
        compiler_params=pltpu.CompilerParams(
            dimension_semantics=("arbitrary", "arbitrary"), vmem_limit_bytes=VMEM_LIMIT),
        name="gla_bidir",
    )(u, u, u, u, u, u, u, u, wup_p, bup_p)


def _merge_kernel(av_ref, ab_ref, ac_ref, az_ref, bz_ref, cz_ref, ga_ref, gb_ref, gc_ref,
                  avp_ref, acp_ref, avn_ref, acn_ref, of_ref, ob_ref, oc_ref, x_ref, mod_ref,
                  cw_ref, gng_ref, wa_ref, wb_ref, wc_ref, wo_ref, out_ref, *, tiles_per_batch):
    tt = pl.program_id(0) % tiles_per_batch
    prev_ok = jnp.logical_and(tt != 0, tt < X_TILES).astype(F32)
    next_ok = (tt < X_TILES - 1).astype(F32)

    f32 = lambda ref: ref[...].astype(F32)
    p = f32(ac_ref) * f32(av_ref)
    p_prev = f32(acp_ref)[HALO - 1:HALO, :] * f32(avp_ref)[HALO - 1:HALO, :] * prev_ok
    p_next = f32(acn_ref)[0:1, :] * f32(avn_ref)[0:1, :] * next_ok
    rows = lax.broadcasted_iota(jnp.int32, p.shape, 0)
    p_up = jnp.where(rows == 0, p_prev, pltpu.roll(p, 1, axis=0))
    p_dn = jnp.where(rows == ROW_TILE - 1, p_next, pltpu.roll(p, ROW_TILE - 1, axis=0))
    cw = cw_ref[...]
    conv = p_up * cw[0:1] + p * cw[1:2] + p_dn * cw[2:3]
    y_a = f32(ab_ref) * conv * _silu(f32(az_ref))
    m = _sigmoid(f32(ga_ref)) * _dot(y_a.astype(BF16), wa_ref[...])

    o = f32(of_ref) + f32(ob_ref)
    gng = gng_ref[...]
    y_b = []
    for h in range(GLA_H):
        sl = slice(h * GLA_DV, (h + 1) * GLA_DV)
        oh = o[:, sl]
        y_b.append(oh * lax.rsqrt(jnp.mean(oh * oh, axis=-1, keepdims=True) + EPS) * gng[:, sl])
    y_b = jnp.concatenate(y_b, axis=-1) * _silu(f32(bz_ref))
    m = m + _sigmoid(f32(gb_ref)) * _dot(y_b.astype(BF16), wb_ref[...])

    y_c = f32(oc_ref) * _silu(f32(cz_ref))
    m = m + _sigmoid(f32(gc_ref)) * _dot(y_c.astype(BF16), wc_ref[...])

    gate = mod_ref[0][:, 2 * D:3 * D]
    out_ref[...] = x_ref[...] + gate * _dot(m.astype(BF16), wo_ref[...])


def _merge(u, o_f, o_b, o_c, xs, mod_l, conv_w, gla_norm_g, wa, wb, wc, wo, layer, last):
    halo_blocks = ROW_TILE // HALO
    n_halo = N_ROWS // HALO
    if last:
        grid = (BATCH * X_TILES,)
        tile = lambda i: (i // X_TILES) * TILES_PER_BATCH + i % X_TILES
        out_rows = BATCH * SEQ
        out_map = lambda i: (i, 0)
    else:
        grid = (N_TILES,)
        tile = lambda i: i
        out_rows = N_ROWS
        out_map = lambda i: (i, 0)

    def col(c):
        return pl.BlockSpec((ROW_TILE, D), lambda i: (tile(i), c))

    def prev_rows(c):
        return pl.BlockSpec((HALO, D), lambda i: (jnp.maximum(tile(i) * halo_blocks - 1, 0), c))

    def next_rows(c):
        return pl.BlockSpec((HALO, D), lambda i: (jnp.minimum((tile(i) + 1) * halo_blocks, n_halo - 1), c))

    row_block = pl.BlockSpec((ROW_TILE, D), lambda i: (tile(i), 0))
    const = lambda i: (0, 0)
    weight = pl.BlockSpec((None, D, D), lambda i: (layer, 0, 0))

    return pl.pallas_call(
        functools.partial(_merge_kernel, tiles_per_batch=X_TILES if last else TILES_PER_BATCH),
        grid=grid,
        in_specs=[col(COL_AV), col(COL_AB), col(COL_AC), col(COL_AZ), col(COL_BZ), col(COL_CZ),
                  col(COL_GA), col(COL_GB), col(COL_GC),
                  prev_rows(COL_AV), prev_rows(COL_AC), next_rows(COL_AV), next_rows(COL_AC),
                  row_block, row_block, row_block, row_block,
                  pl.BlockSpec((1, 1, 3 * D), lambda i: (_tile_group(tile(i)), 0, 0)),
                  pl.BlockSpec((3, D), const), pl.BlockSpec((1, D), const),
                  weight, weight, weight, weight],
        out_specs=pl.BlockSpec((ROW_TILE, D), out_map),
        out_shape=jax.ShapeDtypeStruct((out_rows, D), F32),
        compiler_params=pltpu.CompilerParams(
            dimension_semantics=("arbitrary",), vmem_limit_bytes=VMEM_LIMIT),
        name="merge_last" if last else "merge",
    )(u, u, u, u, u, u, u, u, u, u, u, u, u, o_f, o_b, o_c, xs, mod_l, conv_w, gla_norm_g,
      wa, wb, wc, wo)


def _rope_tables():
    t = np.arange(SEQ)
    row = (t // GRID_W).astype(np.float32)
    colp = (t % GRID_W).astype(np.float32)
    n_freq = MLA_ROPE // 4
    freqs = (np.float32(ROPE_BASE) ** (-np.arange(n_freq, dtype=np.float32) / np.float32(n_freq))).astype(np.float32)
    ang_r = row[:, None] * freqs[None, :]
    ang_c = colp[:, None] * freqs[None, :]
    ang = np.concatenate([ang_r, ang_r, ang_c, ang_c], axis=-1).astype(np.float32)
    cos = np.ones((SB, 128), np.float32)
    sin = np.zeros((SB, 128), np.float32)
    cos[:SEQ, :MLA_ROPE] = np.cos(ang)
    sign = np.where((np.arange(MLA_ROPE) // n_freq) % 2 == 0, -1.0, 1.0).astype(np.float32)
    sin[:SEQ, :MLA_ROPE] = np.sin(ang) * sign[None, :]
    qtab = np.concatenate([cos[:, :MLA_ROPE], sin[:, :MLA_ROPE]], axis=1)
    return jnp.asarray(cos), jnp.asarray(sin), jnp.asarray(qtab)


def _swap16_index():
    l = np.arange(MLA_ROPE)
    return np.where((l // 16) % 2 == 0, l + 16, l - 16)


_W_IN_SEGMENTS = (
    (0, IN_HEAD_COLS),
    (7776, 1024), (8800, 3072),
    (7200, 384), (7584, 128),
    (7712, 64), (7168, 16), (7184, 16),
)
W_PERM_ROWS = 256


def _permute_w_in_kernel(w_ref, o_ref):
    dst = 0
    for src, width in _W_IN_SEGMENTS:
        o_ref[0, :, dst:dst + width] = w_ref[0, :, src:src + width].astype(BF16)
        dst += width
    o_ref[0, :, dst:IN_PAD] = jnp.zeros((W_PERM_ROWS, IN_PAD - dst), BF16)


def _permute_w_in(w_in):
    in_dim = w_in.shape[2]
    return pl.pallas_call(
        _permute_w_in_kernel,
        grid=(DEPTH, D // W_PERM_ROWS),
        in_specs=[pl.BlockSpec((1, W_PERM_ROWS, in_dim), lambda l, r: (l, r, 0))],
        out_specs=pl.BlockSpec((1, W_PERM_ROWS, IN_PAD), lambda l, r: (l, r, 0)),
        out_shape=jax.ShapeDtypeStruct((DEPTH, D, IN_PAD), BF16),
        compiler_params=pltpu.CompilerParams(
            dimension_semantics=("arbitrary", "arbitrary"), vmem_limit_bytes=VMEM_LIMIT),
        name="permute_w_in",
    )(w_in)


def kernel(x, c, ctx, c_ctx, w_mod, b_mod, norm_g, w_in, conv_w, gla_wa_up_f, gla_ba_f, gla_wa_up_b,
           gla_ba_b, gla_norm_g, mla_q_norm_g, mla_kv_norm_g, mla_wq_up, mla_wkv_up, mla_qn_g, mla_kn_g,
           w_br_a, w_br_b, w_br_c, w_out):
    cos_t, sin_t, qtab_t = _rope_tables()
    perm = _swap16_index()
    xs = jnp.concatenate([x, ctx], axis=1).reshape(N_ROWS, D)
    cc = jnp.concatenate([c, c_ctx[None, :], jnp.zeros((8 - BATCH - 1, D), F32)], axis=0)
    mods = _modulation(cc, w_mod, b_mod).reshape(DEPTH, 8, 1, 3 * D)

    hk = GLA_H * GLA_DK
    w_in_p = _permute_w_in(w_in)
    wup_p = jnp.zeros((DEPTH, 128, 2 * hk), F32)
    wup_p = wup_p.at[:, SMALL_AF:SMALL_AF + GLA_RANK, :hk].set(gla_wa_up_f)
    wup_p = wup_p.at[:, SMALL_AB:SMALL_AB + GLA_RANK, hk:].set(gla_wa_up_b).astype(BF16)
    bup_p = jnp.concatenate([gla_ba_f, gla_ba_b], axis=1)[:, None, :]
    wq4 = mla_wq_up.reshape(DEPTH, MLA_Q_RANK, MLA_H, MLA_QK)
    wq_p = jnp.concatenate([wq4, wq4[:, :, :, MLA_NOPE:][:, :, :, perm]], axis=3)
    wq_p = wq_p.reshape(DEPTH, MLA_Q_RANK, MLA_H * MLA_QPAD).astype(BF16)
    qn_p = jnp.concatenate([mla_qn_g, mla_qn_g[:, MLA_NOPE:][:, perm]], axis=1)[:, None, :]
    kn_p = jnp.pad(mla_kn_g, ((0, 0), (0, MLA_QPAD - MLA_QK)))[:, None, :]
    wkv4 = mla_wkv_up.reshape(DEPTH, MLA_KV_RANK, MLA_H, MLA_NOPE + MLA_DV)
    wk = wkv4[:, :, :, :MLA_NOPE].reshape(DEPTH, MLA_KV_RANK, MLA_H * MLA_NOPE).astype(BF16)
    wvt = wkv4[:, :, :, MLA_NOPE:].reshape(DEPTH, MLA_KV_RANK, MLA_H * MLA_DV).transpose(0, 2, 1).astype(BF16)
    wa, wb, wc, wo = (w.astype(BF16) for w in (w_br_a, w_br_b, w_br_c, w_out))

    for l in range(DEPTH):
        last = l == DEPTH - 1
        u = _inproj(xs, mods[l], norm_g[l][None, :], w_in_p, l)
        q, k, vt = _mla_prep(u, cos_t, sin_t, qtab_t, wq_p[l], wk[l], wvt[l],
                             mla_q_norm_g[l][None, :], mla_kv_norm_g[l][None, :], qn_p[l], kn_p[l])
        o_c = _attention(q, k, vt)
        o_f, o_b = _gla(u, wup_p[l], bup_p[l])
        xs = _merge(u, o_f, o_b, o_c, xs, mods[l], conv_w[l], gla_norm_g[l][None, :],
                    wa, wb, wc, wo, l, last)
    return xs.reshape(BATCH, SEQ, D)
```

```python
import functools
import math

import numpy as np
import jax
import jax.numpy as jnp
from jax import lax
from jax.experimental import pallas as pl
from jax.experimental.pallas import tpu as pltpu

D = 1024
BATCH = 4
SEQ = 4096
DEPTH = 4
GRID_W = 64
CTX = 256
EPS = 1e-6

GLA_H = 4
GLA_DK = 128
GLA_DV = 256
GLA_RANK = 16
GLA_TAU = 16.0
CHUNK = 64

MLA_H = 8
MLA_NOPE = 128
MLA_ROPE = 64
MLA_DV = 128
MLA_QK = MLA_NOPE + MLA_ROPE
MLA_QPAD = 256
VT_ROWS = MLA_DV + 16
_SM_SCALE_LOG2E = (MLA_QK ** -0.5) * math.log2(math.e)
MLA_Q_RANK = 384
MLA_KV_RANK = 128
ROPE_BASE = 10000.0

SB = SEQ + CTX
N_ROWS = BATCH * SB
ROW_TILE = 256
TILES_PER_BATCH = SB // ROW_TILE
X_TILES = SEQ // ROW_TILE
N_TILES = N_ROWS // ROW_TILE
CTX_GROUP = BATCH

IN_PAD = 12288
IN_HEAD_COLS = 7168
COL_AV, COL_AB, COL_AC, COL_AZ, _, COL_BV, COL_BZ, COL_CZ, COL_GA, COL_GB, COL_GC = range(11)
COL_BQ, COL_BK, COL_CQKV = 8, 9, 22
COL_SMALL = 92
SMALL_KR, SMALL_AF, SMALL_AB = 0, 64, 80

IN_TN = 2048
IN_TILES_PER_BATCH = 4
IN_TM = SB // IN_TILES_PER_BATCH
Q_TILE = 1024
Q_UNROLL = 4
KV_TILE = 512
ATTN_LOOKAHEAD = 2

VMEM_LIMIT = 56 * 1024 * 1024

F32 = jnp.float32
BF16 = jnp.bfloat16
ACT = BF16
HALO = 16


def _dot(a, b):
    return jnp.dot(a, b, preferred_element_type=F32)


def _dot_nt(a, b):
    return lax.dot_general(a, b, (((1,), (1,)), ((), ())), preferred_element_type=F32)


def _dot_exact(a, b):
    return jnp.dot(a, b, preferred_element_type=F32, precision=lax.Precision.HIGHEST)


def _sigmoid(x):
    return 1.0 / (1.0 + jnp.exp2(x * -math.log2(math.e)))


def _silu(x):
    return x * _sigmoid(x)


def _tile_group(i):
    tt = i % TILES_PER_BATCH
    return jnp.where(tt >= X_TILES, CTX_GROUP, i // TILES_PER_BATCH)


def _mod_kernel(cc_ref, w_ref, b_ref, o_ref):
    o_ref[0] = _dot_exact(_silu(cc_ref[...]), w_ref[0]) + b_ref[0]


def _modulation(cc, w_mod, b_mod):
    return pl.pallas_call(
        _mod_kernel,
        grid=(DEPTH,),
        in_specs=[
            pl.BlockSpec((8, D), lambda l: (0, 0)),
            pl.BlockSpec((1, D, 3 * D), lambda l: (l, 0, 0)),
            pl.BlockSpec((1, 1, 3 * D), lambda l: (l, 0, 0)),
        ],
        out_specs=pl.BlockSpec((1, 8, 3 * D), lambda l: (l, 0, 0)),
        out_shape=jax.ShapeDtypeStruct((DEPTH, 8, 3 * D), F32),
        compiler_params=pltpu.CompilerParams(
            dimension_semantics=("arbitrary",), vmem_limit_bytes=VMEM_LIMIT),
        name="modulation",
    )(cc, w_mod, b_mod.reshape(DEPTH, 1, 3 * D))


def _inproj_kernel(x_ref, modx_ref, modc_ref, g_ref, w_ref, u_ref, h_ref):
    @pl.when(pl.program_id(1) == 0)
    def _():
        x = x_ref[...]
        y = x * lax.rsqrt(jnp.mean(x * x, axis=-1, keepdims=True) + EPS) * g_ref[...]
        row = (pl.program_id(0) % IN_TILES_PER_BATCH) * IN_TM + lax.broadcasted_iota(jnp.int32, x.shape, 0)
        is_ctx = row >= SEQ
        modx = modx_ref[0]
        modc = modc_ref[0]
        scale = jnp.where(is_ctx, modc[:, D:2 * D], modx[:, D:2 * D])
        shift = jnp.where(is_ctx, modc[:, 0:D], modx[:, 0:D])
        h_ref[...] = (y * (1.0 + scale) + shift).astype(BF16)

    u_ref[...] = _dot_nt(h_ref[...], w_ref[...]).astype(u_ref.dtype)


def _inproj(xs, mod_l, norm_g, w_in_p, layer):
    return pl.pallas_call(
        _inproj_kernel,
        grid=(N_ROWS // IN_TM, IN_PAD // IN_TN),
        in_specs=[
            pl.BlockSpec((IN_TM, D), lambda i, j: (i, 0)),
            pl.BlockSpec((1, 1, 3 * D), lambda i, j: (i // IN_TILES_PER_BATCH, 0, 0)),
            pl.BlockSpec((1, 1, 3 * D), lambda i, j: (CTX_GROUP, 0, 0)),
            pl.BlockSpec((1, D), lambda i, j: (0, 0)),
            pl.BlockSpec((None, IN_TN, D), lambda i, j: (layer, j, 0)),
        ],
        out_specs=pl.BlockSpec((IN_TM, IN_TN), lambda i, j: (i, j)),
        out_shape=jax.ShapeDtypeStruct((N_ROWS, IN_PAD), ACT),
        scratch_shapes=[pltpu.VMEM((IN_TM, D), BF16)],
        compiler_params=pltpu.CompilerParams(
            dimension_semantics=("arbitrary", "arbitrary"), vmem_limit_bytes=VMEM_LIMIT),
        name="inproj",
    )(xs, mod_l, mod_l, norm_g, w_in_p)


def _swap16(r):
    lane = lax.broadcasted_iota(jnp.int32, r.shape, 1)
    even = ((lane // 16) % 2) == 0
    return jnp.where(even, pltpu.roll(r, 112, axis=1), pltpu.roll(r, 16, axis=1))


def _mla_prep_kernel(cqkv_ref, small_ref, cos_ref, sin_ref, qtab_ref, wq_ref, wk_ref, wvt_ref, gq_ref, gkv_ref,
                     qn_ref, kn_ref, q_ref, k_ref, vt_ref):
    cqkv = cqkv_ref[...].astype(F32)
    cq = cqkv[:, :MLA_Q_RANK]
    ckv = cqkv[:, MLA_Q_RANK:]

    cq_n = cq * lax.rsqrt(jnp.mean(cq * cq, axis=-1, keepdims=True) + EPS) * gq_ref[...]
    q_all = _dot(cq_n.astype(BF16), wq_ref[...])
    ckv_n = (ckv * lax.rsqrt(jnp.mean(ckv * ckv, axis=-1, keepdims=True) + EPS) * gkv_ref[...]).astype(BF16)
    kn_all = _dot(ckv_n, wk_ref[...])
    vt_all = _dot_nt(wvt_ref[...], ckv_n)

    small = small_ref[...].astype(F32)
    lane = lax.broadcasted_iota(jnp.int32, small.shape, 1)
    kr = jnp.where(lane < MLA_ROPE, small, 0.0)
    kr_ss = jnp.sum(kr * kr, axis=-1, keepdims=True)
    qn_g = qn_ref[...] * _SM_SCALE_LOG2E
    kn_g = kn_ref[...]
    krg = kr * kn_g[:, MLA_NOPE:MLA_QPAD]
    k_rot = krg * cos_ref[...] + _swap16(krg) * sin_ref[...]
    k_rot2 = k_rot + pltpu.roll(k_rot, MLA_ROPE, axis=1)
    qtab = qtab_ref[...] * qn_g[:, MLA_NOPE:MLA_QPAD]
    ones_rows = jnp.ones((VT_ROWS - MLA_DV, ROW_TILE), BF16)

    for h in range(MLA_H):
        q0 = q_all[:, h * MLA_QPAD:h * MLA_QPAD + MLA_NOPE]
        q1 = q_all[:, h * MLA_QPAD + MLA_NOPE:(h + 1) * MLA_QPAD]
        ss = jnp.sum(q0 * q0 + 0.5 * (q1 * q1), axis=-1, keepdims=True)
        inv = lax.rsqrt(ss * (1.0 / MLA_QK) + EPS)
        q_ref[h, :, 0:MLA_NOPE] = (q0 * inv * qn_g[:, 0:MLA_NOPE]).astype(BF16)
        q_ref[h, :, MLA_NOPE:MLA_QPAD] = (q1 * inv * qtab).astype(BF16)

        kn = kn_all[:, h * MLA_NOPE:(h + 1) * MLA_NOPE]
        inv = lax.rsqrt((jnp.sum(kn * kn, axis=-1, keepdims=True) + kr_ss) * (1.0 / MLA_QK) + EPS)
        k_ref[h, :, 0:MLA_NOPE] = (kn * inv * kn_g[:, 0:MLA_NOPE]).astype(BF16)
        k_ref[h, :, MLA_NOPE:MLA_QPAD] = (k_rot2 * inv).astype(BF16)
        vt_ref[h, 0:MLA_DV, :] = vt_all[h * MLA_DV:(h + 1) * MLA_DV, :].astype(BF16)
        vt_ref[h, MLA_DV:VT_ROWS, :] = ones_rows


def _mla_prep(u, cos_t, sin_t, qtab_t, wq_p, wk, wvt, gq, gkv, qn_p, kn_p):
    const = lambda i: (0, 0)
    table = pl.BlockSpec((ROW_TILE, 128), lambda i: (i % TILES_PER_BATCH, 0))
    return pl.pallas_call(
        _mla_prep_kernel,
        grid=(N_TILES,),
        in_specs=[
            pl.BlockSpec((ROW_TILE, 512), lambda i: (i, COL_CQKV)),
            pl.BlockSpec((ROW_TILE, 128), lambda i: (i, COL_SMALL)),
            table, table, table,
            pl.BlockSpec((MLA_Q_RANK, MLA_H * MLA_QPAD), const),
            pl.BlockSpec((MLA_KV_RANK, MLA_H * MLA_NOPE), const),
            pl.BlockSpec((MLA_H * MLA_DV, MLA_KV_RANK), const),
            pl.BlockSpec((1, MLA_Q_RANK), const),
            pl.BlockSpec((1, MLA_KV_RANK), const),
            pl.BlockSpec((1, MLA_QPAD), const),
            pl.BlockSpec((1, MLA_QPAD), const),
        ],
        out_specs=[
            pl.BlockSpec((MLA_H, ROW_TILE, MLA_QPAD), lambda i: (0, i, 0)),
            pl.BlockSpec((MLA_H, ROW_TILE, MLA_QPAD), lambda i: (0, i, 0)),
            pl.BlockSpec((MLA_H, VT_ROWS, ROW_TILE), lambda i: (0, 0, i)),
        ],
        out_shape=[
            jax.ShapeDtypeStruct((MLA_H, N_ROWS, MLA_QPAD), BF16),
            jax.ShapeDtypeStruct((MLA_H, N_ROWS, MLA_QPAD), BF16),
            jax.ShapeDtypeStruct((MLA_H, VT_ROWS, N_ROWS), BF16),
        ],
        compiler_params=pltpu.CompilerParams(
            dimension_semantics=("arbitrary",), vmem_limit_bytes=VMEM_LIMIT),
        name="mla_prep",
    )(u, u, cos_t, sin_t, qtab_t, wq_p, wk, wvt, gq, gkv, qn_p, kn_p)


def _attn_kernel(q_ref, k_ref, vt_ref, o_ref):
    def softmax_tile(q, kv_bounds):
        m = acc = None
        scores = lambda b: _dot_nt(k_ref[0, b[0]:b[1], :], q)
        ahead = [scores(b) for b in kv_bounds[:ATTN_LOOKAHEAD]]
        for c, (lo, hi) in enumerate(kv_bounds):
            s = ahead.pop(0)
            if c + ATTN_LOOKAHEAD < len(kv_bounds):
                ahead.append(scores(kv_bounds[c + ATTN_LOOKAHEAD]))
            m_c = jnp.max(s, axis=0, keepdims=True)
            m_new = m_c if m is None else jnp.maximum(m, m_c)
            p = jnp.exp2(s - m_new)
            pv = _dot(vt_ref[0, :, lo:hi], p.astype(BF16))
            acc = pv if m is None else jnp.exp2(m - m_new) * acc + pv
            m = m_new
        return (acc[0:MLA_DV] / acc[MLA_DV:MLA_DV + 1]).T.astype(o_ref.dtype)

    n_kv = SEQ // KV_TILE
    lat_bounds = [(c * KV_TILE, (c + 1) * KV_TILE) for c in range(n_kv - 1)] + [((n_kv - 1) * KV_TILE, SB)]

    def q_step(i, carry):
        for t in range(Q_UNROLL):
            r0 = pl.multiple_of((i * Q_UNROLL + t) * Q_TILE, Q_TILE)
            o_ref[pl.ds(r0, Q_TILE), :] = softmax_tile(q_ref[0, pl.ds(r0, Q_TILE), :], lat_bounds)
        return carry

    lax.fori_loop(0, SEQ // (Q_TILE * Q_UNROLL), q_step, 0)
    o_ref[SEQ:SB, :] = softmax_tile(q_ref[0, SEQ:SB, :], [(SEQ, SB)])


def _attention(q, k, vt):
    return pl.pallas_call(
        _attn_kernel,
        grid=(BATCH, MLA_H),
        in_specs=[
            pl.BlockSpec((1, SB, MLA_QPAD), lambda b, h: (h, b, 0)),
            pl.BlockSpec((1, SB, MLA_QPAD), lambda b, h: (h, b, 0)),
            pl.BlockSpec((1, VT_ROWS, SB), lambda b, h: (h, 0, b)),
        ],
        out_specs=pl.BlockSpec((SB, MLA_DV), lambda b, h: (b, h)),
        out_shape=jax.ShapeDtypeStruct((N_ROWS, MLA_H * MLA_DV), ACT),
        compiler_params=pltpu.CompilerParams(
            dimension_semantics=("arbitrary", "arbitrary"), vmem_limit_bytes=VMEM_LIMIT),
        name="mla_attention",
    )(q, k, vt)


def _log2_sigmoid(z):
    z2 = z * math.log2(math.e)
    return jnp.minimum(z2, 0.0) - jnp.log2(1.0 + jnp.exp2(-jnp.abs(z2)))


def _gla_kernel(qf_ref, kf_ref, vf_ref, sf_ref, qb_ref, kb_ref, vb_ref, sb_ref, wup_ref, bup_ref,
                of_ref, ob_ref, st_ref):
    @pl.when(pl.program_id(1) == 0)
    def _():
        st_ref[...] = jnp.zeros_like(st_ref)

    hk = GLA_H * GLA_DK
    n_chunks = ROW_TILE // CHUNK
    row = lax.broadcasted_iota(jnp.int32, (ROW_TILE, ROW_TILE), 0)
    col = lax.broadcasted_iota(jnp.int32, (ROW_TILE, ROW_TILE), 1)
    same_chunk = (row // CHUNK) == (col // CHUNK)
    q_scale = GLA_DK ** -0.5

    dirs = ((qf_ref, kf_ref, vf_ref, sf_ref, of_ref), (qb_ref, kb_ref, vb_ref, sb_ref, ob_ref))
    heads = [(d, h) for d in range(2) for h in range(GLA_H)]
    ksl = lambda h: slice(h * GLA_DK, (h + 1) * GLA_DK)
    vsl = lambda h: slice(h * GLA_DV, (h + 1) * GLA_DV)
    csl = lambda c: slice(c * CHUNK, (c + 1) * CHUNK)
    keep = [jnp.logical_and(same_chunk, row >= col), jnp.logical_and(same_chunk, row <= col)]
    scan = [list(range(n_chunks)), list(range(n_chunks - 1, -1, -1))]

    z = [_dot(dirs[d][3][...], wup_ref[:, d * hk:(d + 1) * hk]) + bup_ref[:, d * hk:(d + 1) * hk] for d in range(2)]
    cs = []
    for d in range(2):
        la = _log2_sigmoid(z[d]) * (1.0 / GLA_TAU)
        la_hi = la.astype(BF16)
        la_lo = (la - la_hi.astype(F32)).astype(BF16)
        tri = jnp.where(keep[d], 1.0, 0.0).astype(BF16)
        cs.append(_dot(tri, jnp.concatenate([la_hi, la_lo], axis=1)))

    q_dec, k_inv, k_end, g = [], [], [], []
    for d in range(2):
        q_ref, k_ref = dirs[d][0], dirs[d][1]
        b = cs[d][:, :hk] + cs[d][:, hk:]
        edge = (lambda c: c * CHUNK + CHUNK - 1) if d == 0 else (lambda c: c * CHUNK)
        tots = [b[edge(c):edge(c) + 1] for c in range(n_chunks)]
        tot = jnp.concatenate([jnp.broadcast_to(t, (CHUNK, hk)) for t in tots], axis=0)
        kf = k_ref[...].astype(F32)
        q_dec.append(((q_ref[...].astype(F32) * q_scale) * jnp.exp2(b)).astype(BF16))
        k_inv.append((kf * jnp.exp2(-b)).astype(BF16))
        k_end.append(kf * jnp.exp2(tot - b))
        g.append([jnp.exp2(t) for t in tots])

    att = {(d, h): _dot_nt(q_dec[d][:, ksl(h)], k_inv[d][:, ksl(h)]) for (d, h) in heads}
    upd = {(d, h, c): _dot(k_end[d][csl(c), ksl(h)].T.astype(BF16), dirs[d][2][csl(c), vsl(h)])
           for (d, h) in heads for c in range(n_chunks)}
    o_intra = {(d, h): _dot(jnp.where(keep[d], att[d, h], 0.0).astype(BF16), dirs[d][2][:, vsl(h)])
               for (d, h) in heads}

    states = {}
    for (d, h) in heads:
        st = st_ref[d, h]
        for c in scan[d]:
            states[d, h, c] = st.astype(BF16)
            g_col = jnp.broadcast_to(g[d][c][:, ksl(h)], (GLA_DK, GLA_DK)).T
            st = st * jnp.concatenate([g_col] * (GLA_DV // GLA_DK), axis=1) + upd[d, h, c]
        st_ref[d, h] = st

    for (d, h) in heads:
        o_ref = dirs[d][4]
        for c in range(n_chunks):
            o_inter = _dot(q_dec[d][csl(c), ksl(h)], states[d, h, c])
            o_ref[csl(c), vsl(h)] = (o_intra[d, h][csl(c)] + o_inter).astype(o_ref.dtype)


def _gla(u, wup_p, bup_p):
    def fwd_tile(b, s):
        return b * TILES_PER_BATCH + jnp.where(s == 0, X_TILES, s - 1)

    def bwd_tile(b, s):
        return b * TILES_PER_BATCH + jnp.where(s == 0, X_TILES, X_TILES - s)

    def specs(tile):
        return [
            pl.BlockSpec((ROW_TILE, 512), lambda b, s: (tile(b, s), COL_BQ)),
            pl.BlockSpec((ROW_TILE, 512), lambda b, s: (tile(b, s), COL_BK)),
            pl.BlockSpec((ROW_TILE, 1024), lambda b, s: (tile(b, s), COL_BV)),
            pl.BlockSpec((ROW_TILE, 128), lambda b, s: (tile(b, s), COL_SMALL)),
        ]

    return pl.pallas_call(
        _gla_kernel,
        grid=(BATCH, TILES_PER_BATCH),
        in_specs=specs(fwd_tile) + specs(bwd_tile) + [
            pl.BlockSpec((128, 2 * GLA_H * GLA_DK), lambda b, s: (0, 0)),
            pl.BlockSpec((1, 2 * GLA_H * GLA_DK), lambda b, s: (0, 0)),
        ],
        out_specs=[
            pl.BlockSpec((ROW_TILE, GLA_H * GLA_DV), lambda b, s: (fwd_tile(b, s), 0)),
            pl.BlockSpec((ROW_TILE, GLA_H * GLA_DV), lambda b, s: (bwd_tile(b, s), 0)),
        ],
        out_shape=[jax.ShapeDtypeStruct((N_ROWS, GLA_H * GLA_DV), ACT)] * 2,
        scratch_shapes=[pltpu.VMEM((2, GLA_H, GLA_DK, GLA_DV), F32)],
        compiler_params=pltpu.CompilerParams(
            dimension_semantics=("arbitrary", "arbitrary"), vmem_limit_bytes=VMEM_LIMIT),
        name="gla_bidir",
    )(u, u, u, u, u, u, u, u, wup_p, bup_p)


def _merge_kernel(av_ref, ab_ref, ac_ref, az_ref, bz_ref, cz_ref, ga_ref, gb_ref, gc_ref,
                  avp_ref, acp_ref, avn_ref, acn_ref, of_ref, ob_ref, oc_ref, x_ref, mod_ref,
                  cw_ref, gng_ref, wa_ref, wb_ref, wc_ref, wo_ref, out_ref, *, tiles_per_batch):
    tt = pl.program_id(0) % tiles_per_batch
    prev_ok = jnp.logical_and(tt != 0, tt < X_TILES).astype(F32)
    next_ok = (tt < X_TILES - 1).astype(F32)

    f32 = lambda ref: ref[...].astype(F32)
    p = f32(ac_ref) * f32(av_ref)
    p_prev = f32(acp_ref)[HALO - 1:HALO, :] * f32(avp_ref)[HALO - 1:HALO, :] * prev_ok
    p_next = f32(acn_ref)[0:1, :] * f32(avn_ref)[0:1, :] * next_ok
    rows = lax.broadcasted_iota(jnp.int32, p.shape, 0)
    p_up = jnp.where(rows == 0, p_prev, pltpu.roll(p, 1, axis=0))
    p_dn = jnp.where(rows == ROW_TILE - 1, p_next, pltpu.roll(p, ROW_TILE - 1, axis=0))
    cw = cw_ref[...]
    conv = p_up * cw[0:1] + p * cw[1:2] + p_dn * cw[2:3]
    y_a = f32(ab_ref) * conv * _silu(f32(az_ref))
    m = _sigmoid(f32(ga_ref)) * _dot(y_a.astype(BF16), wa_ref[...])

    o = f32(of_ref) + f32(ob_ref)
    gng = gng_ref[...]
    y_b = []
    for h in range(GLA_H):
        sl = slice(h * GLA_DV, (h + 1) * GLA_DV)
        oh = o[:, sl]
        y_b.append(oh * lax.rsqrt(jnp.mean(oh * oh, axis=-1, keepdims=True) + EPS) * gng[:, sl])
    y_b = jnp.concatenate(y_b, axis=-1) * _silu(f32(bz_ref))
    m = m + _sigmoid(f32(gb_ref)) * _dot(y_b.astype(BF16), wb_ref[...])

    y_c = f32(oc_ref) * _silu(f32(cz_ref))
    m = m + _sigmoid(f32(gc_ref)) * _dot(y_c.astype(BF16), wc_ref[...])

    gate = mod_ref[0][:, 2 * D:3 * D]
    out_ref[...] = x_ref[...] + gate * _dot(m.astype(BF16), wo_ref[...])


def _merge(u, o_f, o_b, o_c, xs, mod_l, conv_w, gla_norm_g, wa, wb, wc, wo, layer, last):
    halo_blocks = ROW_TILE // HALO
    n_halo = N_ROWS // HALO
    if last:
        grid = (BATCH * X_TILES,)
        tile = lambda i: (i // X_TILES) * TILES_PER_BATCH + i % X_TILES
        out_rows = BATCH * SEQ
        out_map = lambda i: (i, 0)
    else:
        grid = (N_TILES,)
        tile = lambda i: i
        out_rows = N_ROWS
        out_map = lambda i: (i, 0)

    def col(c):
        return pl.BlockSpec((ROW_TILE, D), lambda i: (tile(i), c))

    def prev_rows(c):
        return pl.BlockSpec((HALO, D), lambda i: (jnp.maximum(tile(i) * halo_blocks - 1, 0), c))

    def next_rows(c):
        return pl.BlockSpec((HALO, D), lambda i: (jnp.minimum((tile(i) + 1) * halo_blocks, n_halo - 1), c))

    row_block = pl.BlockSpec((ROW_TILE, D), lambda i: (tile(i), 0))
    const = lambda i: (0, 0)
    weight = pl.BlockSpec((None, D, D), lambda i: (layer, 0, 0))

    return pl.pallas_call(
        functools.partial(_merge_kernel, tiles_per_batch=X_TILES if last else TILES_PER_BATCH),
        grid=grid,
        in_specs=[col(COL_AV), col(COL_AB), col(COL_AC), col(COL_AZ), col(COL_BZ), col(COL_CZ),
                  col(COL_GA), col(COL_GB), col(COL_GC),
                  prev_rows(COL_AV), prev_rows(COL_AC), next_rows(COL_AV), next_rows(COL_AC),
                  row_block, row_block, row_block, row_block,
                  pl.BlockSpec((1, 1, 3 * D), lambda i: (_tile_group(tile(i)), 0, 0)),
                  pl.BlockSpec((3, D), const), pl.BlockSpec((1, D), const),
                  weight, weight, weight, weight],
        out_specs=pl.BlockSpec((ROW_TILE, D), out_map),
        out_shape=jax.ShapeDtypeStruct((out_rows, D), F32),
        compiler_params=pltpu.CompilerParams(
            dimension_semantics=("arbitrary",), vmem_limit_bytes=VMEM_LIMIT),
        name="merge_last" if last else "merge",
    )(u, u, u, u, u, u, u, u, u, u, u, u, u, o_f, o_b, o_c, xs, mod_l, conv_w, gla_norm_g,
      wa, wb, wc, wo)


def _rope_tables():
    t = np.arange(SEQ)
    row = (t // GRID_W).astype(np.float32)
    colp = (t % GRID_W).astype(np.float32)
    n_freq = MLA_ROPE // 4
    freqs = (np.float32(ROPE_BASE) ** (-np.arange(n_freq, dtype=np.float32) / np.float32(n_freq))).astype(np.float32)
    ang_r = row[:, None] * freqs[None, :]
    ang_c = colp[:, None] * freqs[None, :]
    ang = np.concatenate([ang_r, ang_r, ang_c, ang_c], axis=-1).astype(np.float32)
    cos = np.ones((SB, 128), np.float32)
    sin = np.zeros((SB, 128), np.float32)
    cos[:SEQ, :MLA_ROPE] = np.cos(ang)
    sign = np.where((np.arange(MLA_ROPE) // n_freq) % 2 == 0, -1.0, 1.0).astype(np.float32)
    sin[:SEQ, :MLA_ROPE] = np.sin(ang) * sign[None, :]
    qtab = np.concatenate([cos[:, :MLA_ROPE], sin[:, :MLA_ROPE]], axis=1)
    return jnp.asarray(cos), jnp.asarray(sin), jnp.asarray(qtab)


def _swap16_index():
    l = np.arange(MLA_ROPE)
    return np.where((l // 16) % 2 == 0, l + 16, l - 16)


WP = 1024
W_SPLIT = 7776 - IN_HEAD_COLS
W_SMALL = 7200 - IN_HEAD_COLS
N_WBLOCKS = IN_PAD // WP
HEAD_WBLOCKS = IN_HEAD_COLS // WP


def _permute_w_in_kernel(a_ref, b_ref, o_ref):
    blk = pl.program_id(1)

    @pl.when(blk < HEAD_WBLOCKS)
    def _():
        o_ref[0] = a_ref[0].astype(BF16)

    @pl.when(jnp.logical_and(blk >= HEAD_WBLOCKS, blk < N_WBLOCKS - 1))
    def _():
        o_ref[0, 0:WP - W_SPLIT, :] = a_ref[0, W_SPLIT:WP, :].astype(BF16)
        o_ref[0, WP - W_SPLIT:WP, :] = b_ref[0, 0:W_SPLIT, :].astype(BF16)

    @pl.when(blk == N_WBLOCKS - 1)
    def _():
        o_ref[0, 0:W_SPLIT - W_SMALL, :] = a_ref[0, W_SMALL:W_SPLIT, :].astype(BF16)
        o_ref[0, W_SPLIT - W_SMALL:W_SPLIT, :] = a_ref[0, 0:W_SMALL, :].astype(BF16)
        o_ref[0, W_SPLIT:WP, :] = jnp.zeros((WP - W_SPLIT, D), BF16)


def _permute_w_in(w_in):
    wt = jnp.transpose(w_in, (0, 2, 1))
    last = N_WBLOCKS - 1
    a_map = lambda l, b: (l, jnp.where(b == last, HEAD_WBLOCKS, b), 0)
    b_map = lambda l, b: (l, jnp.clip(b + 1, HEAD_WBLOCKS, last), 0)
    return pl.pallas_call(
        _permute_w_in_kernel,
        grid=(DEPTH, N_WBLOCKS),
        in_specs=[pl.BlockSpec((1, WP, D), a_map), pl.BlockSpec((1, WP, D), b_map)],
        out_specs=pl.BlockSpec((1, WP, D), lambda l, b: (l, b, 0)),
        out_shape=jax.ShapeDtypeStruct((DEPTH, IN_PAD, D), BF16),
        compiler_params=pltpu.CompilerParams(
            dimension_semantics=("arbitrary", "arbitrary"), vmem_limit_bytes=VMEM_LIMIT),
        name="permute_w_in",
    )(wt, wt)


def kernel(x, c, ctx, c_ctx, w_mod, b_mod, norm_g, w_in, conv_w, gla_wa_up_f, gla_ba_f, gla_wa_up_b,
           gla_ba_b, gla_norm_g, mla_q_norm_g, mla_kv_norm_g, mla_wq_up, mla_wkv_up, mla_qn_g, mla_kn_g,
           w_br_a, w_br_b, w_br_c, w_out):
    cos_t, sin_t, qtab_t = _rope_tables()
    perm = _swap16_index()
    xs = jnp.concatenate([x, ctx], axis=1).reshape(N_ROWS, D)
    cc = jnp.concatenate([c, c_ctx[None, :], jnp.zeros((8 - BATCH - 1, D), F32)], axis=0)
    mods = _modulation(cc, w_mod, b_mod).reshape(DEPTH, 8, 1, 3 * D)

    hk = GLA_H * GLA_DK
    w_in_p = _permute_w_in(w_in)
    wup_p = jnp.zeros((DEPTH, 128, 2 * hk), F32)
    wup_p = wup_p.at[:, SMALL_AF:SMALL_AF + GLA_RANK, :hk].set(gla_wa_up_f)
    wup_p = wup_p.at[:, SMALL_AB:SMALL_AB + GLA_RANK, hk:].set(gla_wa_up_b).astype(BF16)
    bup_p = jnp.concatenate([gla_ba_f, gla_ba_b], axis=1)[:, None, :]
    wq4 = mla_wq_up.reshape(DEPTH, MLA_Q_RANK, MLA_H, MLA_QK)
    wq_p = jnp.concatenate([wq4, wq4[:, :, :, MLA_NOPE:][:, :, :, perm]], axis=3)
    wq_p = wq_p.reshape(DEPTH, MLA_Q_RANK, MLA_H * MLA_QPAD).astype(BF16)
    qn_p = jnp.concatenate([mla_qn_g, mla_qn_g[:, MLA_NOPE:][:, perm]], axis=1)[:, None, :]
    kn_p = jnp.pad(mla_kn_g, ((0, 0), (0, MLA_QPAD - MLA_QK)))[:, None, :]
    wkv4 = mla_wkv_up.reshape(DEPTH, MLA_KV_RANK, MLA_H, MLA_NOPE + MLA_DV)
    wk = wkv4[:, :, :, :MLA_NOPE].reshape(DEPTH, MLA_KV_RANK, MLA_H * MLA_NOPE).astype(BF16)
    wvt = wkv4[:, :, :, MLA_NOPE:].reshape(DEPTH, MLA_KV_RANK, MLA_H * MLA_DV).transpose(0, 2, 1).astype(BF16)
    wa, wb, wc, wo = (w.astype(BF16) for w in (w_br_a, w_br_b, w_br_c, w_out))

    for l in range(DEPTH):
        last = l == DEPTH - 1
        u = _inproj(xs, mods[l], norm_g[l][None, :], w_in_p, l)
        q, k, vt = _mla_prep(u, cos_t, sin_t, qtab_t, wq_p[l], wk[l], wvt[l],
                             mla_q_norm_g[l][None, :], mla_kv_norm_g[l][None, :], qn_p[l], kn_p[l])
        o_c = _attention(q, k, vt)
        o_f, o_b = _gla(u, wup_p[l], bup_p[l])
        xs = _merge(u, o_f, o_b, o_c, xs, mods[l], conv_w[l], gla_norm_g[l][None, :],
                    wa, wb, wc, wo, l, last)
    return xs.reshape(BATCH, SEQ, D)
```

```python
import functools
import math

import numpy as np
import jax
import jax.numpy as jnp
from jax import lax
from jax.experimental import pallas as pl
from jax.experimental.pallas import tpu as pltpu

D = 1024
BATCH = 4
SEQ = 4096
DEPTH = 4
GRID_W = 64
CTX = 256
EPS = 1e-6

GLA_H = 4
GLA_DK = 128
GLA_DV = 256
GLA_RANK = 16
GLA_TAU = 16.0
CHUNK = 64

MLA_H = 8
MLA_NOPE = 128
MLA_ROPE = 64
MLA_DV = 128
MLA_QK = MLA_NOPE + MLA_ROPE
MLA_QPAD = 256
VT_ROWS = MLA_DV + 16
_SM_SCALE_LOG2E = (MLA_QK ** -0.5) * math.log2(math.e)
MLA_Q_RANK = 384
MLA_KV_RANK = 128
ROPE_BASE = 10000.0

SB = SEQ + CTX
N_ROWS = BATCH * SB
ROW_TILE = 256
TILES_PER_BATCH = SB // ROW_TILE
X_TILES = SEQ // ROW_TILE
N_TILES = N_ROWS // ROW_TILE
CTX_GROUP = BATCH

IN_PAD = 12288
IN_HEAD_COLS = 7168
COL_AV, COL_AB, COL_AC, COL_AZ, _, COL_BV, COL_BZ, COL_CZ, COL_GA, COL_GB, COL_GC = range(11)
COL_BQ, COL_BK, COL_CQKV = 8, 9, 22
COL_SMALL = 92
SMALL_KR, SMALL_AF, SMALL_AB = 0, 64, 80

IN_TN = 2048
IN_TILES_PER_BATCH = 4
IN_TM = SB // IN_TILES_PER_BATCH
Q_TILE = 1024
Q_UNROLL = 4
KV_TILE = 512
ATTN_LOOKAHEAD = 2

VMEM_LIMIT = 56 * 1024 * 1024

F32 = jnp.float32
BF16 = jnp.bfloat16
ACT = BF16
HALO = 16


def _dot(a, b):
    return jnp.dot(a, b, preferred_element_type=F32)


def _dot_nt(a, b):
    return lax.dot_general(a, b, (((1,), (1,)), ((), ())), preferred_element_type=F32)


def _dot_exact(a, b):
    return jnp.dot(a, b, preferred_element_type=F32, precision=lax.Precision.HIGHEST)


def _sigmoid(x):
    return 1.0 / (1.0 + jnp.exp2(x * -math.log2(math.e)))


def _silu(x):
    return x * _sigmoid(x)


def _tile_group(i):
    tt = i % TILES_PER_BATCH
    return jnp.where(tt >= X_TILES, CTX_GROUP, i // TILES_PER_BATCH)


def _mod_kernel(cc_ref, w_ref, b_ref, o_ref):
    o_ref[0] = _dot_exact(_silu(cc_ref[...]), w_ref[0]) + b_ref[0]


def _modulation(cc, w_mod, b_mod):
    return pl.pallas_call(
        _mod_kernel,
        grid=(DEPTH,),
        in_specs=[
            pl.BlockSpec((8, D), lambda l: (0, 0)),
            pl.BlockSpec((1, D, 3 * D), lambda l: (l, 0, 0)),
            pl.BlockSpec((1, 1, 3 * D), lambda l: (l, 0, 0)),
        ],
        out_specs=pl.BlockSpec((1, 8, 3 * D), lambda l: (l, 0, 0)),
        out_shape=jax.ShapeDtypeStruct((DEPTH, 8, 3 * D), F32),
        compiler_params=pltpu.CompilerParams(
            dimension_semantics=("arbitrary",), vmem_limit_bytes=VMEM_LIMIT),
        name="modulation",
    )(cc, w_mod, b_mod.reshape(DEPTH, 1, 3 * D))


def _inproj_kernel(x_ref, modx_ref, modc_ref, g_ref, w_ref, u_ref, h_ref):
    @pl.when(pl.program_id(1) == 0)
    def _():
        x = x_ref[...]
        y = x * lax.rsqrt(jnp.mean(x * x, axis=-1, keepdims=True) + EPS) * g_ref[...]
        row = (pl.program_id(0) % IN_TILES_PER_BATCH) * IN_TM + lax.broadcasted_iota(jnp.int32, x.shape, 0)
        is_ctx = row >= SEQ
        modx = modx_ref[0]
        modc = modc_ref[0]
        scale = jnp.where(is_ctx, modc[:, D:2 * D], modx[:, D:2 * D])
        shift = jnp.where(is_ctx, modc[:, 0:D], modx[:, 0:D])
        h_ref[...] = (y * (1.0 + scale) + shift).astype(BF16)

    u_ref[...] = _dot_nt(h_ref[...], w_ref[...]).astype(u_ref.dtype)


def _inproj(xs, mod_l, norm_g, w_in_p, layer):
    return pl.pallas_call(
        _inproj_kernel,
        grid=(N_ROWS // IN_TM, IN_PAD // IN_TN),
        in_specs=[
            pl.BlockSpec((IN_TM, D), lambda i, j: (i, 0)),
            pl.BlockSpec((1, 1, 3 * D), lambda i, j: (i // IN_TILES_PER_BATCH, 0, 0)),
            pl.BlockSpec((1, 1, 3 * D), lambda i, j: (CTX_GROUP, 0, 0)),
            pl.BlockSpec((1, D), lambda i, j: (0, 0)),
            pl.BlockSpec((None, IN_TN, D), lambda i, j: (layer, j, 0)),
        ],
        out_specs=pl.BlockSpec((IN_TM, IN_TN), lambda i, j: (i, j)),
        out_shape=jax.ShapeDtypeStruct((N_ROWS, IN_PAD), ACT),
        scratch_shapes=[pltpu.VMEM((IN_TM, D), BF16)],
        compiler_params=pltpu.CompilerParams(
            dimension_semantics=("arbitrary", "arbitrary"), vmem_limit_bytes=VMEM_LIMIT),
        name="inproj",
    )(xs, mod_l, mod_l, norm_g, w_in_p)


def _swap16(r):
    lane = lax.broadcasted_iota(jnp.int32, r.shape, 1)
    even = ((lane // 16) % 2) == 0
    return jnp.where(even, pltpu.roll(r, 112, axis=1), pltpu.roll(r, 16, axis=1))


def _mla_project(cqkv_ref, wq_ref, wk_ref, wvt_ref, gq_ref, gkv_ref):
    cqkv = cqkv_ref[...].astype(F32)
    cq = cqkv[:, :MLA_Q_RANK]
    ckv = cqkv[:, MLA_Q_RANK:]

    cq_n = cq * lax.rsqrt(jnp.mean(cq * cq, axis=-1, keepdims=True) + EPS) * gq_ref[...]
    q_all = _dot(cq_n.astype(BF16), wq_ref[...])
    ckv_n = (ckv * lax.rsqrt(jnp.mean(ckv * ckv, axis=-1, keepdims=True) + EPS) * gkv_ref[...]).astype(BF16)
    kn_all = _dot(ckv_n, wk_ref[...])
    vt_all = _dot_nt(wvt_ref[...], ckv_n)
    return q_all, kn_all, vt_all


def _mla_finish(q_all, kn_all, vt_all, small_ref, cos_ref, sin_ref, qtab_ref, qn_ref, kn_ref, q_ref, k_ref, vt_ref):
    small = small_ref[...].astype(F32)
    lane = lax.broadcasted_iota(jnp.int32, small.shape, 1)
    kr = jnp.where(lane < MLA_ROPE, small, 0.0)
    kr_ss = jnp.sum(kr * kr, axis=-1, keepdims=True)
    qn_g = qn_ref[...] * _SM_SCALE_LOG2E
    kn_g = kn_ref[...]
    krg = kr * kn_g[:, MLA_NOPE:MLA_QPAD]
    k_rot = krg * cos_ref[...] + _swap16(krg) * sin_ref[...]
    k_rot2 = k_rot + pltpu.roll(k_rot, MLA_ROPE, axis=1)
    qtab = qtab_ref[...] * qn_g[:, MLA_NOPE:MLA_QPAD]
    ones_rows = jnp.ones((VT_ROWS - MLA_DV, ROW_TILE), BF16)

    for h in range(MLA_H):
        q0 = q_all[:, h * MLA_QPAD:h * MLA_QPAD + MLA_NOPE]
        q1 = q_all[:, h * MLA_QPAD + MLA_NOPE:(h + 1) * MLA_QPAD]
        ss = jnp.sum(q0 * q0 + 0.5 * (q1 * q1), axis=-1, keepdims=True)
        inv = lax.rsqrt(ss * (1.0 / MLA_QK) + EPS)
        q_ref[h, :, 0:MLA_NOPE] = (q0 * inv * qn_g[:, 0:MLA_NOPE]).astype(BF16)
        q_ref[h, :, MLA_NOPE:MLA_QPAD] = (q1 * inv * qtab).astype(BF16)

        kn = kn_all[:, h * MLA_NOPE:(h + 1) * MLA_NOPE]
        inv = lax.rsqrt((jnp.sum(kn * kn, axis=-1, keepdims=True) + kr_ss) * (1.0 / MLA_QK) + EPS)
        k_ref[h, :, 0:MLA_NOPE] = (kn * inv * kn_g[:, 0:MLA_NOPE]).astype(BF16)
        k_ref[h, :, MLA_NOPE:MLA_QPAD] = (k_rot2 * inv).astype(BF16)
        vt_ref[h, 0:MLA_DV, :] = vt_all[h * MLA_DV:(h + 1) * MLA_DV, :].astype(BF16)
        vt_ref[h, MLA_DV:VT_ROWS, :] = ones_rows


def _attn_kernel(q_ref, k_ref, vt_ref, o_ref):
    def softmax_tile(q, kv_bounds):
        m = acc = None
        scores = lambda b: _dot_nt(k_ref[0, b[0]:b[1], :], q)
        ahead = [scores(b) for b in kv_bounds[:ATTN_LOOKAHEAD]]
        for c, (lo, hi) in enumerate(kv_bounds):
            s = ahead.pop(0)
            if c + ATTN_LOOKAHEAD < len(kv_bounds):
                ahead.append(scores(kv_bounds[c + ATTN_LOOKAHEAD]))
            m_c = jnp.max(s, axis=0, keepdims=True)
            m_new = m_c if m is None else jnp.maximum(m, m_c)
            p = jnp.exp2(s - m_new)
            pv = _dot(vt_ref[0, :, lo:hi], p.astype(BF16))
            acc = pv if m is None else jnp.exp2(m - m_new) * acc + pv
            m = m_new
        return (acc[0:MLA_DV] / acc[MLA_DV:MLA_DV + 1]).T.astype(o_ref.dtype)

    n_kv = SEQ // KV_TILE
    lat_bounds = [(c * KV_TILE, (c + 1) * KV_TILE) for c in range(n_kv - 1)] + [((n_kv - 1) * KV_TILE, SB)]

    def q_step(i, carry):
        for t in range(Q_UNROLL):
            r0 = pl.multiple_of((i * Q_UNROLL + t) * Q_TILE, Q_TILE)
            o_ref[pl.ds(r0, Q_TILE), :] = softmax_tile(q_ref[0, pl.ds(r0, Q_TILE), :], lat_bounds)
        return carry

    lax.fori_loop(0, SEQ // (Q_TILE * Q_UNROLL), q_step, 0)
    o_ref[SEQ:SB, :] = softmax_tile(q_ref[0, SEQ:SB, :], [(SEQ, SB)])


def _attention(q, k, vt):
    return pl.pallas_call(
        _attn_kernel,
        grid=(BATCH, MLA_H),
        in_specs=[
            pl.BlockSpec((1, SB, MLA_QPAD), lambda b, h: (h, b, 0)),
            pl.BlockSpec((1, SB, MLA_QPAD), lambda b, h: (h, b, 0)),
            pl.BlockSpec((1, VT_ROWS, SB), lambda b, h: (h, 0, b)),
        ],
        out_specs=pl.BlockSpec((SB, MLA_DV), lambda b, h: (b, h)),
        out_shape=jax.ShapeDtypeStruct((N_ROWS, MLA_H * MLA_DV), ACT),
        compiler_params=pltpu.CompilerParams(
            dimension_semantics=("arbitrary", "arbitrary"), vmem_limit_bytes=VMEM_LIMIT),
        name="mla_attention",
    )(q, k, vt)


def _log2_sigmoid(z):
    z2 = z * math.log2(math.e)
    return jnp.minimum(z2, 0.0) - jnp.log2(1.0 + jnp.exp2(-jnp.abs(z2)))


def _gla_kernel(qf_ref, kf_ref, vf_ref, sf_ref, qb_ref, kb_ref, vb_ref, sb_ref, wup_ref, bup_ref,
                cqkv_ref, cos_ref, sin_ref, qtab_ref, wq_ref, wk_ref, wvt_ref, gq_ref, gkv_ref, qn_ref, kn_ref,
                of_ref, ob_ref, mq_ref, mk_ref, mvt_ref, st_ref):
    @pl.when(pl.program_id(1) == 0)
    def _():
        st_ref[...] = jnp.zeros_like(st_ref)

    mla = _mla_project(cqkv_ref, wq_ref, wk_ref, wvt_ref, gq_ref, gkv_ref)

    hk = GLA_H * GLA_DK
    n_chunks = ROW_TILE // CHUNK
    row = lax.broadcasted_iota(jnp.int32, (ROW_TILE, ROW_TILE), 0)
    col = lax.broadcasted_iota(jnp.int32, (ROW_TILE, ROW_TILE), 1)
    same_chunk = (row // CHUNK) == (col // CHUNK)
    q_scale = GLA_DK ** -0.5

    dirs = ((qf_ref, kf_ref, vf_ref, sf_ref, of_ref), (qb_ref, kb_ref, vb_ref, sb_ref, ob_ref))
    heads = [(d, h) for d in range(2) for h in range(GLA_H)]
    ksl = lambda h: slice(h * GLA_DK, (h + 1) * GLA_DK)
    vsl = lambda h: slice(h * GLA_DV, (h + 1) * GLA_DV)
    csl = lambda c: slice(c * CHUNK, (c + 1) * CHUNK)
    keep = [jnp.logical_and(same_chunk, row >= col), jnp.logical_and(same_chunk, row <= col)]
    scan = [list(range(n_chunks)), list(range(n_chunks - 1, -1, -1))]

    z = [_dot(dirs[d][3][...], wup_ref[:, d * hk:(d + 1) * hk]) + bup_ref[:, d * hk:(d + 1) * hk] for d in range(2)]
    cs = []
    for d in range(2):
        la = _log2_sigmoid(z[d]) * (1.0 / GLA_TAU)
        la_hi = la.astype(BF16)
        la_lo = (la - la_hi.astype(F32)).astype(BF16)
        tri = jnp.where(keep[d], 1.0, 0.0).astype(BF16)
        cs.append(_dot(tri, jnp.concatenate([la_hi, la_lo], axis=1)))

    q_dec, k_inv, k_end, g = [], [], [], []
    for d in range(2):
        q_ref, k_ref = dirs[d][0], dirs[d][1]
        b = cs[d][:, :hk] + cs[d][:, hk:]
        edge = (lambda c: c * CHUNK + CHUNK - 1) if d == 0 else (lambda c: c * CHUNK)
        tots = [b[edge(c):edge(c) + 1] for c in range(n_chunks)]
        tot = jnp.concatenate([jnp.broadcast_to(t, (CHUNK, hk)) for t in tots], axis=0)
        kf = k_ref[...].astype(F32)
        q_dec.append(((q_ref[...].astype(F32) * q_scale) * jnp.exp2(b)).astype(BF16))
        k_inv.append((kf * jnp.exp2(-b)).astype(BF16))
        k_end.append(kf * jnp.exp2(tot - b))
        g.append([jnp.exp2(t) for t in tots])

    att = {(d, h): _dot_nt(q_dec[d][:, ksl(h)], k_inv[d][:, ksl(h)]) for (d, h) in heads}
    upd = {(d, h, c): _dot(k_end[d][csl(c), ksl(h)].T.astype(BF16), dirs[d][2][csl(c), vsl(h)])
           for (d, h) in heads for c in range(n_chunks)}
    o_intra = {(d, h): _dot(jnp.where(keep[d], att[d, h], 0.0).astype(BF16), dirs[d][2][:, vsl(h)])
               for (d, h) in heads}

    _mla_finish(*mla, sf_ref, cos_ref, sin_ref, qtab_ref, qn_ref, kn_ref, mq_ref, mk_ref, mvt_ref)

    states = {}
    for (d, h) in heads:
        st = st_ref[d, h]
        for c in scan[d]:
            states[d, h, c] = st.astype(BF16)
            g_col = jnp.broadcast_to(g[d][c][:, ksl(h)], (GLA_DK, GLA_DK)).T
            st = st * jnp.concatenate([g_col] * (GLA_DV // GLA_DK), axis=1) + upd[d, h, c]
        st_ref[d, h] = st

    for (d, h) in heads:
        o_ref = dirs[d][4]
        for c in range(n_chunks):
            o_inter = _dot(q_dec[d][csl(c), ksl(h)], states[d, h, c])
            o_ref[csl(c), vsl(h)] = (o_intra[d, h][csl(c)] + o_inter).astype(o_ref.dtype)


def _gla_and_mla_prep(u, wup_p, bup_p, cos_t, sin_t, qtab_t, wq_p, wk, wvt, gq, gkv, qn_p, kn_p):
    def fwd_pos(s):
        return jnp.where(s == 0, X_TILES, s - 1)

    def fwd_tile(b, s):
        return b * TILES_PER_BATCH + fwd_pos(s)

    def bwd_tile(b, s):
        return b * TILES_PER_BATCH + jnp.where(s == 0, X_TILES, X_TILES - s)

    def specs(tile):
        return [
            pl.BlockSpec((ROW_TILE, 512), lambda b, s: (tile(b, s), COL_BQ)),
            pl.BlockSpec((ROW_TILE, 512), lambda b, s: (tile(b, s), COL_BK)),
            pl.BlockSpec((ROW_TILE, 1024), lambda b, s: (tile(b, s), COL_BV)),
            pl.BlockSpec((ROW_TILE, 128), lambda b, s: (tile(b, s), COL_SMALL)),
        ]

    const = lambda b, s: (0, 0)
    table = pl.BlockSpec((ROW_TILE, 128), lambda b, s: (fwd_pos(s), 0))
    return pl.pallas_call(
        _gla_kernel,
        grid=(BATCH, TILES_PER_BATCH),
        in_specs=specs(fwd_tile) + specs(bwd_tile) + [
            pl.BlockSpec((128, 2 * GLA_H * GLA_DK), const),
            pl.BlockSpec((1, 2 * GLA_H * GLA_DK), const),
            pl.BlockSpec((ROW_TILE, 512), lambda b, s: (fwd_tile(b, s), COL_CQKV)),
            table, table, table,
            pl.BlockSpec((MLA_Q_RANK, MLA_H * MLA_QPAD), const),
            pl.BlockSpec((MLA_KV_RANK, MLA_H * MLA_NOPE), const),
            pl.BlockSpec((MLA_H * MLA_DV, MLA_KV_RANK), const),
            pl.BlockSpec((1, MLA_Q_RANK), const),
            pl.BlockSpec((1, MLA_KV_RANK), const),
            pl.BlockSpec((1, MLA_QPAD), const),
            pl.BlockSpec((1, MLA_QPAD), const),
        ],
        out_specs=[
            pl.BlockSpec((ROW_TILE, GLA_H * GLA_DV), lambda b, s: (fwd_tile(b, s), 0)),
            pl.BlockSpec((ROW_TILE, GLA_H * GLA_DV), lambda b, s: (bwd_tile(b, s), 0)),
            pl.BlockSpec((MLA_H, ROW_TILE, MLA_QPAD), lambda b, s: (0, fwd_tile(b, s), 0)),
            pl.BlockSpec((MLA_H, ROW_TILE, MLA_QPAD), lambda b, s: (0, fwd_tile(b, s), 0)),
            pl.BlockSpec((MLA_H, VT_ROWS, ROW_TILE), lambda b, s: (0, 0, fwd_tile(b, s))),
        ],
        out_shape=[
            jax.ShapeDtypeStruct((N_ROWS, GLA_H * GLA_DV), ACT),
            jax.ShapeDtypeStruct((N_ROWS, GLA_H * GLA_DV), ACT),
            jax.ShapeDtypeStruct((MLA_H, N_ROWS, MLA_QPAD), BF16),
            jax.ShapeDtypeStruct((MLA_H, N_ROWS, MLA_QPAD), BF16),
            jax.ShapeDtypeStruct((MLA_H, VT_ROWS, N_ROWS), BF16),
        ],
        scratch_shapes=[pltpu.VMEM((2, GLA_H, GLA_DK, GLA_DV), F32)],
        compiler_params=pltpu.CompilerParams(
            dimension_semantics=("arbitrary", "arbitrary"), vmem_limit_bytes=VMEM_LIMIT),
        name="gla_bidir_mla_prep",
    )(u, u, u, u, u, u, u, u, wup_p, bup_p, u, cos_t, sin_t, qtab_t, wq_p, wk, wvt, gq, gkv, qn_p, kn_p)


def _merge_kernel(av_ref, ab_ref, ac_ref, az_ref, bz_ref, cz_ref, ga_ref, gb_ref, gc_ref,
                  avp_ref, acp_ref, avn_ref, acn_ref, of_ref, ob_ref, oc_ref, x_ref, mod_ref,
                  cw_ref, gng_ref, wa_ref, wb_ref, wc_ref, wo_ref, out_ref, *, tiles_per_batch):
    tt = pl.program_id(0) % tiles_per_batch
    prev_ok = jnp.logical_and(tt != 0, tt < X_TILES).astype(F32)
    next_ok = (tt < X_TILES - 1).astype(F32)

    f32 = lambda ref: ref[...].astype(F32)
    p = f32(ac_ref) * f32(av_ref)
    p_prev = f32(acp_ref)[HALO - 1:HALO, :] * f32(avp_ref)[HALO - 1:HALO, :] * prev_ok
    p_next = f32(acn_ref)[0:1, :] * f32(avn_ref)[0:1, :] * next_ok
    rows = lax.broadcasted_iota(jnp.int32, p.shape, 0)
    p_up = jnp.where(rows == 0, p_prev, pltpu.roll(p, 1, axis=0))
    p_dn = jnp.where(rows == ROW_TILE - 1, p_next, pltpu.roll(p, ROW_TILE - 1, axis=0))
    cw = cw_ref[...]
    conv = p_up * cw[0:1] + p * cw[1:2] + p_dn * cw[2:3]
    y_a = f32(ab_ref) * conv * _silu(f32(az_ref))
    m = _sigmoid(f32(ga_ref)) * _dot(y_a.astype(BF16), wa_ref[...])

    o = f32(of_ref) + f32(ob_ref)
    gng = gng_ref[...]
    y_b = []
    for h in range(GLA_H):
        sl = slice(h * GLA_DV, (h + 1) * GLA_DV)
        oh = o[:, sl]
        y_b.append(oh * lax.rsqrt(jnp.mean(oh * oh, axis=-1, keepdims=True) + EPS) * gng[:, sl])
    y_b = jnp.concatenate(y_b, axis=-1) * _silu(f32(bz_ref))
    m = m + _sigmoid(f32(gb_ref)) * _dot(y_b.astype(BF16), wb_ref[...])

    y_c = f32(oc_ref) * _silu(f32(cz_ref))
    m = m + _sigmoid(f32(gc_ref)) * _dot(y_c.astype(BF16), wc_ref[...])

    gate = mod_ref[0][:, 2 * D:3 * D]
    out_ref[...] = x_ref[...] + gate * _dot(m.astype(BF16), wo_ref[...])


def _merge(u, o_f, o_b, o_c, xs, mod_l, conv_w, gla_norm_g, wa, wb, wc, wo, layer, last):
    halo_blocks = ROW_TILE // HALO
    n_halo = N_ROWS // HALO
    if last:
        grid = (BATCH * X_TILES,)
        tile = lambda i: (i // X_TILES) * TILES_PER_BATCH + i % X_TILES
        out_rows = BATCH * SEQ
        out_map = lambda i: (i, 0)
    else:
        grid = (N_TILES,)
        tile = lambda i: i
        out_rows = N_ROWS
        out_map = lambda i: (i, 0)

    def col(c):
        return pl.BlockSpec((ROW_TILE, D), lambda i: (tile(i), c))

    def prev_rows(c):
        return pl.BlockSpec((HALO, D), lambda i: (jnp.maximum(tile(i) * halo_blocks - 1, 0), c))

    def next_rows(c):
        return pl.BlockSpec((HALO, D), lambda i: (jnp.minimum((tile(i) + 1) * halo_blocks, n_halo - 1), c))

    row_block = pl.BlockSpec((ROW_TILE, D), lambda i: (tile(i), 0))
    const = lambda i: (0, 0)
    weight = pl.BlockSpec((None, D, D), lambda i: (layer, 0, 0))

    return pl.pallas_call(
        functools.partial(_merge_kernel, tiles_per_batch=X_TILES if last else TILES_PER_BATCH),
        grid=grid,
        in_specs=[col(COL_AV), col(COL_AB), col(COL_AC), col(COL_AZ), col(COL_BZ), col(COL_CZ),
                  col(COL_GA), col(COL_GB), col(COL_GC),
                  prev_rows(COL_AV), prev_rows(COL_AC), next_rows(COL_AV), next_rows(COL_AC),
                  row_block, row_block, row_block, row_block,
                  pl.BlockSpec((1, 1, 3 * D), lambda i: (_tile_group(tile(i)), 0, 0)),
                  pl.BlockSpec((3, D), const), pl.BlockSpec((1, D), const),
                  weight, weight, weight, weight],
        out_specs=pl.BlockSpec((ROW_TILE, D), out_map),
        out_shape=jax.ShapeDtypeStruct((out_rows, D), F32),
        compiler_params=pltpu.CompilerParams(
            dimension_semantics=("arbitrary",), vmem_limit_bytes=VMEM_LIMIT),
        name="merge_last" if last else "merge",
    )(u, u, u, u, u, u, u, u, u, u, u, u, u, o_f, o_b, o_c, xs, mod_l, conv_w, gla_norm_g,
      wa, wb, wc, wo)


def _rope_tables():
    t = np.arange(SEQ)
    row = (t // GRID_W).astype(np.float32)
    colp = (t % GRID_W).astype(np.float32)
    n_freq = MLA_ROPE // 4
    freqs = (np.float32(ROPE_BASE) ** (-np.arange(n_freq, dtype=np.float32) / np.float32(n_freq))).astype(np.float32)
    ang_r = row[:, None] * freqs[None, :]
    ang_c = colp[:, None] * freqs[None, :]
    ang = np.concatenate([ang_r, ang_r, ang_c, ang_c], axis=-1).astype(np.float32)
    cos = np.ones((SB, 128), np.float32)
    sin = np.zeros((SB, 128), np.float32)
    cos[:SEQ, :MLA_ROPE] = np.cos(ang)
    sign = np.where((np.arange(MLA_ROPE) // n_freq) % 2 == 0, -1.0, 1.0).astype(np.float32)
    sin[:SEQ, :MLA_ROPE] = np.sin(ang) * sign[None, :]
    qtab = np.concatenate([cos[:, :MLA_ROPE], sin[:, :MLA_ROPE]], axis=1)
    return jnp.asarray(cos), jnp.asarray(sin), jnp.asarray(qtab)


def _swap16_index():
    l = np.arange(MLA_ROPE)
    return np.where((l // 16) % 2 == 0, l + 16, l - 16)


WP = 1024
W_SPLIT = 7776 - IN_HEAD_COLS
W_SMALL = 7200 - IN_HEAD_COLS
N_WBLOCKS = IN_PAD // WP
HEAD_WBLOCKS = IN_HEAD_COLS // WP


def _permute_w_in_kernel(a_ref, b_ref, o_ref):
    blk = pl.program_id(1)

    @pl.when(blk < HEAD_WBLOCKS)
    def _():
        o_ref[0] = a_ref[0].astype(BF16)

    @pl.when(jnp.logical_and(blk >= HEAD_WBLOCKS, blk < N_WBLOCKS - 1))
    def _():
        o_ref[0, 0:WP - W_SPLIT, :] = a_ref[0, W_SPLIT:WP, :].astype(BF16)
        o_ref[0, WP - W_SPLIT:WP, :] = b_ref[0, 0:W_SPLIT, :].astype(BF16)

    @pl.when(blk == N_WBLOCKS - 1)
    def _():
        o_ref[0, 0:W_SPLIT - W_SMALL, :] = a_ref[0, W_SMALL:W_SPLIT, :].astype(BF16)
        o_ref[0, W_SPLIT - W_SMALL:W_SPLIT, :] = a_ref[0, 0:W_SMALL, :].astype(BF16)
        o_ref[0, W_SPLIT:WP, :] = jnp.zeros((WP - W_SPLIT, D), BF16)


def _permute_w_in(w_in):
    wt = jnp.transpose(w_in, (0, 2, 1))
    last = N_WBLOCKS - 1
    a_map = lambda l, b: (l, jnp.where(b == last, HEAD_WBLOCKS, b), 0)
    b_map = lambda l, b: (l, jnp.clip(b + 1, HEAD_WBLOCKS, last), 0)
    return pl.pallas_call(
        _permute_w_in_kernel,
        grid=(DEPTH, N_WBLOCKS),
        in_specs=[pl.BlockSpec((1, WP, D), a_map), pl.BlockSpec((1, WP, D), b_map)],
        out_specs=pl.BlockSpec((1, WP, D), lambda l, b: (l, b, 0)),
        out_shape=jax.ShapeDtypeStruct((DEPTH, IN_PAD, D), BF16),
        compiler_params=pltpu.CompilerParams(
            dimension_semantics=("arbitrary", "arbitrary"), vmem_limit_bytes=VMEM_LIMIT),
        name="permute_w_in",
    )(wt, wt)


def kernel(x, c, ctx, c_ctx, w_mod, b_mod, norm_g, w_in, conv_w, gla_wa_up_f, gla_ba_f, gla_wa_up_b,
           gla_ba_b, gla_norm_g, mla_q_norm_g, mla_kv_norm_g, mla_wq_up, mla_wkv_up, mla_qn_g, mla_kn_g,
           w_br_a, w_br_b, w_br_c, w_out):
    cos_t, sin_t, qtab_t = _rope_tables()
    perm = _swap16_index()
    xs = jnp.concatenate([x, ctx], axis=1).reshape(N_ROWS, D)
    cc = jnp.concatenate([c, c_ctx[None, :], jnp.zeros((8 - BATCH - 1, D), F32)], axis=0)
    mods = _modulation(cc, w_mod, b_mod).reshape(DEPTH, 8, 1, 3 * D)

    hk = GLA_H * GLA_DK
    w_in_p = _permute_w_in(w_in)
    wup_p = jnp.zeros((DEPTH, 128, 2 * hk), F32)
    wup_p = wup_p.at[:, SMALL_AF:SMALL_AF + GLA_RANK, :hk].set(gla_wa_up_f)
    wup_p = wup_p.at[:, SMALL_AB:SMALL_AB + GLA_RANK, hk:].set(gla_wa_up_b).astype(BF16)
    bup_p = jnp.concatenate([gla_ba_f, gla_ba_b], axis=1)[:, None, :]
    wq4 = mla_wq_up.reshape(DEPTH, MLA_Q_RANK, MLA_H, MLA_QK)
    wq_p = jnp.concatenate([wq4, wq4[:, :, :, MLA_NOPE:][:, :, :, perm]], axis=3)
    wq_p = wq_p.reshape(DEPTH, MLA_Q_RANK, MLA_H * MLA_QPAD).astype(BF16)
    qn_p = jnp.concatenate([mla_qn_g, mla_qn_g[:, MLA_NOPE:][:, perm]], axis=1)[:, None, :]
    kn_p = jnp.pad(mla_kn_g, ((0, 0), (0, MLA_QPAD - MLA_QK)))[:, None, :]
    wkv4 = mla_wkv_up.reshape(DEPTH, MLA_KV_RANK, MLA_H, MLA_NOPE + MLA_DV)
    wk = wkv4[:, :, :, :MLA_NOPE].reshape(DEPTH, MLA_KV_RANK, MLA_H * MLA_NOPE).astype(BF16)
    wvt = wkv4[:, :, :, MLA_NOPE:].reshape(DEPTH, MLA_KV_RANK, MLA_H * MLA_DV).transpose(0, 2, 1).astype(BF16)
    wa, wb, wc, wo = (w.astype(BF16) for w in (w_br_a, w_br_b, w_br_c, w_out))

    for l in range(DEPTH):
        last = l == DEPTH - 1
        u = _inproj(xs, mods[l], norm_g[l][None, :], w_in_p, l)
        o_f, o_b, q, k, vt = _gla_and_mla_prep(
            u, wup_p[l], bup_p[l], cos_t, sin_t, qtab_t, wq_p[l], wk[l], wvt[l],
            mla_q_norm_g[l][None, :], mla_kv_norm_g[l][None, :], qn_p[l], kn_p[l])
        o_c = _attention(q, k, vt)
        xs = _merge(u, o_f, o_b, o_c, xs, mods[l], conv_w[l], gla_norm_g[l][None, :],
                    wa, wb, wc, wo, l, last)
    return xs.reshape(BATCH, SEQ, D)
```

```python
import functools
import math

import numpy as np
import jax
import jax.numpy as jnp
from jax import lax
from jax.experimental import pallas as pl
from jax.experimental.pallas import tpu as pltpu

D = 1024
BATCH = 4
SEQ = 4096
DEPTH = 4
GRID_W = 64
CTX = 256
EPS = 1e-6

GLA_H = 4
GLA_DK = 128
GLA_DV = 256
GLA_RANK = 16
GLA_TAU = 16.0
CHUNK = 64

MLA_H = 8
MLA_NOPE = 128
MLA_ROPE = 64
MLA_DV = 128
MLA_QK = MLA_NOPE + MLA_ROPE
MLA_QPAD = 256
VT_ROWS = MLA_DV + 16
_SM_SCALE_LOG2E = (MLA_QK ** -0.5) * math.log2(math.e)
MLA_Q_RANK = 384
MLA_KV_RANK = 128
ROPE_BASE = 10000.0

SB = SEQ + CTX
N_ROWS = BATCH * SB
ROW_TILE = 256
TILES_PER_BATCH = SB // ROW_TILE
X_TILES = SEQ // ROW_TILE
N_TILES = N_ROWS // ROW_TILE
CTX_GROUP = BATCH

IN_PAD = 12288
IN_HEAD_COLS = 7168
COL_AV, COL_AB, COL_AC, COL_AZ, _, COL_BV, COL_BZ, COL_CZ, COL_GA, COL_GB, COL_GC = range(11)
COL_BQ, COL_BK, COL_CQKV = 8, 9, 22
COL_SMALL = 92
SMALL_KR, SMALL_AF, SMALL_AB = 0, 64, 80

IN_TN = 2048
IN_PREP_ROWS = 272
IN_TILES_PER_BATCH = 4
IN_TM = SB // IN_TILES_PER_BATCH
Q_TILE = 1024
Q_UNROLL = 4
KV_TILE = 512
ATTN_LOOKAHEAD = 2

VMEM_LIMIT = 56 * 1024 * 1024

F32 = jnp.float32
BF16 = jnp.bfloat16
ACT = BF16
HALO = 16


def _dot(a, b):
    return jnp.dot(a, b, preferred_element_type=F32)


def _dot_nt(a, b):
    return lax.dot_general(a, b, (((1,), (1,)), ((), ())), preferred_element_type=F32)


def _dot_exact(a, b):
    return jnp.dot(a, b, preferred_element_type=F32, precision=lax.Precision.HIGHEST)


def _sigmoid(x):
    return 1.0 / (1.0 + jnp.exp2(x * -math.log2(math.e)))


def _silu(x):
    return x * _sigmoid(x)


def _tile_group(i):
    tt = i % TILES_PER_BATCH
    return jnp.where(tt >= X_TILES, CTX_GROUP, i // TILES_PER_BATCH)


def _mod_kernel(cc_ref, w_ref, b_ref, o_ref):
    o_ref[0] = _dot_exact(_silu(cc_ref[...]), w_ref[0]) + b_ref[0]


def _modulation(cc, w_mod, b_mod):
    return pl.pallas_call(
        _mod_kernel,
        grid=(DEPTH,),
        in_specs=[
            pl.BlockSpec((8, D), lambda l: (0, 0)),
            pl.BlockSpec((1, D, 3 * D), lambda l: (l, 0, 0)),
            pl.BlockSpec((1, 1, 3 * D), lambda l: (l, 0, 0)),
        ],
        out_specs=pl.BlockSpec((1, 8, 3 * D), lambda l: (l, 0, 0)),
        out_shape=jax.ShapeDtypeStruct((DEPTH, 8, 3 * D), F32),
        compiler_params=pltpu.CompilerParams(
            dimension_semantics=("arbitrary",), vmem_limit_bytes=VMEM_LIMIT),
        name="modulation",
    )(cc, w_mod, b_mod.reshape(DEPTH, 1, 3 * D))


def _inproj_kernel(x_ref, modx_ref, modc_ref, g_ref, w_ref, u_ref, h_ref):
    @pl.when(pl.program_id(1) == 0)
    def _():
        modx = modx_ref[0]
        modc = modc_ref[0]
        for r in range(0, IN_TM, IN_PREP_ROWS):
            x = x_ref[r:r + IN_PREP_ROWS, :]
            y = x * lax.rsqrt(jnp.mean(x * x, axis=-1, keepdims=True) + EPS) * g_ref[...]
            row = ((pl.program_id(0) % IN_TILES_PER_BATCH) * IN_TM + r
                   + lax.broadcasted_iota(jnp.int32, x.shape, 0))
            is_ctx = row >= SEQ
            scale = jnp.where(is_ctx, modc[:, D:2 * D], modx[:, D:2 * D])
            shift = jnp.where(is_ctx, modc[:, 0:D], modx[:, 0:D])
            h = (y * (1.0 + scale) + shift).astype(BF16)
            h_ref[r:r + IN_PREP_ROWS, :] = h
            u_ref[r:r + IN_PREP_ROWS, :] = _dot_nt(h, w_ref[...]).astype(u_ref.dtype)

    @pl.when(pl.program_id(1) > 0)
    def _():
        u_ref[...] = _dot_nt(h_ref[...], w_ref[...]).astype(u_ref.dtype)


def _inproj(xs, mod_l, norm_g, w_in_p, layer):
    return pl.pallas_call(
        _inproj_kernel,
        grid=(N_ROWS // IN_TM, IN_PAD // IN_TN),
        in_specs=[
            pl.BlockSpec((IN_TM, D), lambda i, j: (i, 0)),
            pl.BlockSpec((1, 1, 3 * D), lambda i, j: (i // IN_TILES_PER_BATCH, 0, 0)),
            pl.BlockSpec((1, 1, 3 * D), lambda i, j: (CTX_GROUP, 0, 0)),
            pl.BlockSpec((1, D), lambda i, j: (0, 0)),
            pl.BlockSpec((None, IN_TN, D), lambda i, j: (layer, j, 0)),
        ],
        out_specs=pl.BlockSpec((IN_TM, IN_TN), lambda i, j: (i, j)),
        out_shape=jax.ShapeDtypeStruct((N_ROWS, IN_PAD), ACT),
        scratch_shapes=[pltpu.VMEM((IN_TM, D), BF16)],
        compiler_params=pltpu.CompilerParams(
            dimension_semantics=("arbitrary", "arbitrary"), vmem_limit_bytes=VMEM_LIMIT),
        name="inproj",
    )(xs, mod_l, mod_l, norm_g, w_in_p)


def _swap16(r):
    lane = lax.broadcasted_iota(jnp.int32, r.shape, 1)
    even = ((lane // 16) % 2) == 0
    return jnp.where(even, pltpu.roll(r, 112, axis=1), pltpu.roll(r, 16, axis=1))


def _mla_project(cqkv_ref, wq_ref, wk_ref, wvt_ref, gq_ref, gkv_ref):
    cqkv = cqkv_ref[...].astype(F32)
    cq = cqkv[:, :MLA_Q_RANK]
    ckv = cqkv[:, MLA_Q_RANK:]

    cq_n = cq * lax.rsqrt(jnp.mean(cq * cq, axis=-1, keepdims=True) + EPS) * gq_ref[...]
    q_all = _dot(cq_n.astype(BF16), wq_ref[...])
    ckv_n = (ckv * lax.rsqrt(jnp.mean(ckv * ckv, axis=-1, keepdims=True) + EPS) * gkv_ref[...]).astype(BF16)
    kn_all = _dot(ckv_n, wk_ref[...])
    vt_all = _dot_nt(wvt_ref[...], ckv_n)
    return q_all, kn_all, vt_all


def _mla_finish(q_all, kn_all, vt_all, small_ref, cos_ref, sin_ref, qtab_ref, qn_ref, kn_ref, q_ref, k_ref, vt_ref):
    small = small_ref[...].astype(F32)
    lane = lax.broadcasted_iota(jnp.int32, small.shape, 1)
    kr = jnp.where(lane < MLA_ROPE, small, 0.0)
    kr_ss = jnp.sum(kr * kr, axis=-1, keepdims=True)
    qn_g = qn_ref[...] * _SM_SCALE_LOG2E
    kn_g = kn_ref[...]
    krg = kr * kn_g[:, MLA_NOPE:MLA_QPAD]
    k_rot = krg * cos_ref[...] + _swap16(krg) * sin_ref[...]
    k_rot2 = k_rot + pltpu.roll(k_rot, MLA_ROPE, axis=1)
    qtab = qtab_ref[...] * qn_g[:, MLA_NOPE:MLA_QPAD]
    ones_rows = jnp.ones((VT_ROWS - MLA_DV, ROW_TILE), BF16)

    for h in range(MLA_H):
        q0 = q_all[:, h * MLA_QPAD:h * MLA_QPAD + MLA_NOPE]
        q1 = q_all[:, h * MLA_QPAD + MLA_NOPE:(h + 1) * MLA_QPAD]
        ss = jnp.sum(q0 * q0 + 0.5 * (q1 * q1), axis=-1, keepdims=True)
        inv = lax.rsqrt(ss * (1.0 / MLA_QK) + EPS)
        q_ref[h, :, 0:MLA_NOPE] = (q0 * inv * qn_g[:, 0:MLA_NOPE]).astype(BF16)
        q_ref[h, :, MLA_NOPE:MLA_QPAD] = (q1 * inv * qtab).astype(BF16)

        kn = kn_all[:, h * MLA_NOPE:(h + 1) * MLA_NOPE]
        inv = lax.rsqrt((jnp.sum(kn * kn, axis=-1, keepdims=True) + kr_ss) * (1.0 / MLA_QK) + EPS)
        k_ref[h, :, 0:MLA_NOPE] = (kn * inv * kn_g[:, 0:MLA_NOPE]).astype(BF16)
        k_ref[h, :, MLA_NOPE:MLA_QPAD] = (k_rot2 * inv).astype(BF16)
        vt_ref[h, 0:MLA_DV, :] = vt_all[h * MLA_DV:(h + 1) * MLA_DV, :].astype(BF16)
        vt_ref[h, MLA_DV:VT_ROWS, :] = ones_rows


def _attn_kernel(q_ref, k_ref, vt_ref, o_ref):
    def softmax_tile(q, kv_bounds):
        m = acc = None
        scores = lambda b: _dot_nt(k_ref[0, b[0]:b[1], :], q)
        ahead = [scores(b) for b in kv_bounds[:ATTN_LOOKAHEAD]]
        for c, (lo, hi) in enumerate(kv_bounds):
            s = ahead.pop(0)
            if c + ATTN_LOOKAHEAD < len(kv_bounds):
                ahead.append(scores(kv_bounds[c + ATTN_LOOKAHEAD]))
            m_c = jnp.max(s, axis=0, keepdims=True)
            m_new = m_c if m is None else jnp.maximum(m, m_c)
            p = jnp.exp2(s - m_new)
            pv = _dot(vt_ref[0, :, lo:hi], p.astype(BF16))
            acc = pv if m is None else jnp.exp2(m - m_new) * acc + pv
            m = m_new
        return (acc[0:MLA_DV] / acc[MLA_DV:MLA_DV + 1]).T.astype(o_ref.dtype)

    n_kv = SEQ // KV_TILE
    lat_bounds = [(c * KV_TILE, (c + 1) * KV_TILE) for c in range(n_kv - 1)] + [((n_kv - 1) * KV_TILE, SB)]

    def q_step(i, carry):
        for t in range(Q_UNROLL):
            r0 = pl.multiple_of((i * Q_UNROLL + t) * Q_TILE, Q_TILE)
            o_ref[pl.ds(r0, Q_TILE), :] = softmax_tile(q_ref[0, pl.ds(r0, Q_TILE), :], lat_bounds)
        return carry

    lax.fori_loop(0, SEQ // (Q_TILE * Q_UNROLL), q_step, 0)
    o_ref[SEQ:SB, :] = softmax_tile(q_ref[0, SEQ:SB, :], [(SEQ, SB)])


def _attention(q, k, vt):
    return pl.pallas_call(
        _attn_kernel,
        grid=(BATCH, MLA_H),
        in_specs=[
            pl.BlockSpec((1, SB, MLA_QPAD), lambda b, h: (h, b, 0)),
            pl.BlockSpec((1, SB, MLA_QPAD), lambda b, h: (h, b, 0)),
            pl.BlockSpec((1, VT_ROWS, SB), lambda b, h: (h, 0, b)),
        ],
        out_specs=pl.BlockSpec((SB, MLA_DV), lambda b, h: (b, h)),
        out_shape=jax.ShapeDtypeStruct((N_ROWS, MLA_H * MLA_DV), ACT),
        compiler_params=pltpu.CompilerParams(
            dimension_semantics=("arbitrary", "arbitrary"), vmem_limit_bytes=VMEM_LIMIT),
        name="mla_attention",
    )(q, k, vt)


def _log2_sigmoid(z):
    z2 = z * math.log2(math.e)
    return jnp.minimum(z2, 0.0) - jnp.log2(1.0 + jnp.exp2(-jnp.abs(z2)))


def _gla_kernel(qf_ref, kf_ref, vf_ref, sf_ref, qb_ref, kb_ref, vb_ref, sb_ref, wup_ref, bup_ref,
                cqkv_ref, cos_ref, sin_ref, qtab_ref, wq_ref, wk_ref, wvt_ref, gq_ref, gkv_ref, qn_ref, kn_ref,
                of_ref, ob_ref, mq_ref, mk_ref, mvt_ref, st_ref):
    @pl.when(pl.program_id(1) == 0)
    def _():
        st_ref[...] = jnp.zeros_like(st_ref)

    mla = _mla_project(cqkv_ref, wq_ref, wk_ref, wvt_ref, gq_ref, gkv_ref)

    hk = GLA_H * GLA_DK
    n_chunks = ROW_TILE // CHUNK
    row = lax.broadcasted_iota(jnp.int32, (ROW_TILE, ROW_TILE), 0)
    col = lax.broadcasted_iota(jnp.int32, (ROW_TILE, ROW_TILE), 1)
    same_chunk = (row // CHUNK) == (col // CHUNK)
    q_scale = GLA_DK ** -0.5

    dirs = ((qf_ref, kf_ref, vf_ref, sf_ref, of_ref), (qb_ref, kb_ref, vb_ref, sb_ref, ob_ref))
    heads = [(d, h) for d in range(2) for h in range(GLA_H)]
    ksl = lambda h: slice(h * GLA_DK, (h + 1) * GLA_DK)
    vsl = lambda h: slice(h * GLA_DV, (h + 1) * GLA_DV)
    csl = lambda c: slice(c * CHUNK, (c + 1) * CHUNK)
    keep = [jnp.logical_and(same_chunk, row >= col), jnp.logical_and(same_chunk, row <= col)]
    scan = [list(range(n_chunks)), list(range(n_chunks - 1, -1, -1))]

    z = [_dot(dirs[d][3][...], wup_ref[:, d * hk:(d + 1) * hk]) + bup_ref[:, d * hk:(d + 1) * hk] for d in range(2)]
    cs = []
    for d in range(2):
        la = _log2_sigmoid(z[d]) * (1.0 / GLA_TAU)
        la_hi = la.astype(BF16)
        la_lo = (la - la_hi.astype(F32)).astype(BF16)
        tri = jnp.where(keep[d], 1.0, 0.0).astype(BF16)
        cs.append(_dot(tri, jnp.concatenate([la_hi, la_lo], axis=1)))

    q_dec, k_inv, k_end, g = [], [], [], []
    for d in range(2):
        q_ref, k_ref = dirs[d][0], dirs[d][1]
        b = cs[d][:, :hk] + cs[d][:, hk:]
        edge = (lambda c: c * CHUNK + CHUNK - 1) if d == 0 else (lambda c: c * CHUNK)
        tots = [b[edge(c):edge(c) + 1] for c in range(n_chunks)]
        tot = jnp.concatenate([jnp.broadcast_to(t, (CHUNK, hk)) for t in tots], axis=0)
        kf = k_ref[...].astype(F32)
        q_dec.append(((q_ref[...].astype(F32) * q_scale) * jnp.exp2(b)).astype(BF16))
        k_inv.append((kf * jnp.exp2(-b)).astype(BF16))
        k_end.append(kf * jnp.exp2(tot - b))
        g.append([jnp.exp2(t) for t in tots])

    att = {(d, h): _dot_nt(q_dec[d][:, ksl(h)], k_inv[d][:, ksl(h)]) for (d, h) in heads}
    upd = {(d, h, c): _dot(k_end[d][csl(c), ksl(h)].T.astype(BF16), dirs[d][2][csl(c), vsl(h)])
           for (d, h) in heads for c in range(n_chunks)}
    o_intra = {(d, h): _dot(jnp.where(keep[d], att[d, h], 0.0).astype(BF16), dirs[d][2][:, vsl(h)])
               for (d, h) in heads}

    _mla_finish(*mla, sf_ref, cos_ref, sin_ref, qtab_ref, qn_ref, kn_ref, mq_ref, mk_ref, mvt_ref)

    states = {}
    for (d, h) in heads:
        st = st_ref[d, h]
        for c in scan[d]:
            states[d, h, c] = st.astype(BF16)
            g_col = jnp.broadcast_to(g[d][c][:, ksl(h)], (GLA_DK, GLA_DK)).T
            st = st * jnp.concatenate([g_col] * (GLA_DV // GLA_DK), axis=1) + upd[d, h, c]
        st_ref[d, h] = st

    for (d, h) in heads:
        o_ref = dirs[d][4]
        for c in range(n_chunks):
            o_inter = _dot(q_dec[d][csl(c), ksl(h)], states[d, h, c])
            o_ref[csl(c), vsl(h)] = (o_intra[d, h][csl(c)] + o_inter).astype(o_ref.dtype)


def _gla_and_mla_prep(u, wup_p, bup_p, cos_t, sin_t, qtab_t, wq_p, wk, wvt, gq, gkv, qn_p, kn_p):
    def fwd_pos(s):
        return jnp.where(s == 0, X_TILES, s - 1)

    def fwd_tile(b, s):
        return b * TILES_PER_BATCH + fwd_pos(s)

    def bwd_tile(b, s):
        return b * TILES_PER_BATCH + jnp.where(s == 0, X_TILES, X_TILES - s)

    def specs(tile):
        return [
            pl.BlockSpec((ROW_TILE, 512), lambda b, s: (tile(b, s), COL_BQ)),
            pl.BlockSpec((ROW_TILE, 512), lambda b, s: (tile(b, s), COL_BK)),
            pl.BlockSpec((ROW_TILE, 1024), lambda b, s: (tile(b, s), COL_BV)),
            pl.BlockSpec((ROW_TILE, 128), lambda b, s: (tile(b, s), COL_SMALL)),
        ]

    const = lambda b, s: (0, 0)
    table = pl.BlockSpec((ROW_TILE, 128), lambda b, s: (fwd_pos(s), 0))
    return pl.pallas_call(
        _gla_kernel,
        grid=(BATCH, TILES_PER_BATCH),
        in_specs=specs(fwd_tile) + specs(bwd_tile) + [
            pl.BlockSpec((128, 2 * GLA_H * GLA_DK), const),
            pl.BlockSpec((1, 2 * GLA_H * GLA_DK), const),
            pl.BlockSpec((ROW_TILE, 512), lambda b, s: (fwd_tile(b, s), COL_CQKV)),
            table, table, table,
            pl.BlockSpec((MLA_Q_RANK, MLA_H * MLA_QPAD), const),
            pl.BlockSpec((MLA_KV_RANK, MLA_H * MLA_NOPE), const),
            pl.BlockSpec((MLA_H * MLA_DV, MLA_KV_RANK), const),
            pl.BlockSpec((1, MLA_Q_RANK), const),
            pl.BlockSpec((1, MLA_KV_RANK), const),
            pl.BlockSpec((1, MLA_QPAD), const),
            pl.BlockSpec((1, MLA_QPAD), const),
        ],
        out_specs=[
            pl.BlockSpec((ROW_TILE, GLA_H * GLA_DV), lambda b, s: (fwd_tile(b, s), 0)),
            pl.BlockSpec((ROW_TILE, GLA_H * GLA_DV), lambda b, s: (bwd_tile(b, s), 0)),
            pl.BlockSpec((MLA_H, ROW_TILE, MLA_QPAD), lambda b, s: (0, fwd_tile(b, s), 0)),
            pl.BlockSpec((MLA_H, ROW_TILE, MLA_QPAD), lambda b, s: (0, fwd_tile(b, s), 0)),
            pl.BlockSpec((MLA_H, VT_ROWS, ROW_TILE), lambda b, s: (0, 0, fwd_tile(b, s))),
        ],
        out_shape=[
            jax.ShapeDtypeStruct((N_ROWS, GLA_H * GLA_DV), ACT),
            jax.ShapeDtypeStruct((N_ROWS, GLA_H * GLA_DV), ACT),
            jax.ShapeDtypeStruct((MLA_H, N_ROWS, MLA_QPAD), BF16),
            jax.ShapeDtypeStruct((MLA_H, N_ROWS, MLA_QPAD), BF16),
            jax.ShapeDtypeStruct((MLA_H, VT_ROWS, N_ROWS), BF16),
        ],
        scratch_shapes=[pltpu.VMEM((2, GLA_H, GLA_DK, GLA_DV), F32)],
        compiler_params=pltpu.CompilerParams(
            dimension_semantics=("arbitrary", "arbitrary"), vmem_limit_bytes=VMEM_LIMIT),
        name="gla_bidir_mla_prep",
    )(u, u, u, u, u, u, u, u, wup_p, bup_p, u, cos_t, sin_t, qtab_t, wq_p, wk, wvt, gq, gkv, qn_p, kn_p)


def _merge_kernel(av_ref, ab_ref, ac_ref, az_ref, bz_ref, cz_ref, ga_ref, gb_ref, gc_ref,
                  avp_ref, acp_ref, avn_ref, acn_ref, of_ref, ob_ref, oc_ref, x_ref, mod_ref,
                  cw_ref, gng_ref, wa_ref, wb_ref, wc_ref, wo_ref, out_ref, *, tiles_per_batch):
    tt = pl.program_id(0) % tiles_per_batch
    prev_ok = jnp.logical_and(tt != 0, tt < X_TILES).astype(F32)
    next_ok = (tt < X_TILES - 1).astype(F32)

    f32 = lambda ref: ref[...].astype(F32)
    p = f32(ac_ref) * f32(av_ref)
    p_prev = f32(acp_ref)[HALO - 1:HALO, :] * f32(avp_ref)[HALO - 1:HALO, :] * prev_ok
    p_next = f32(acn_ref)[0:1, :] * f32(avn_ref)[0:1, :] * next_ok
    rows = lax.broadcasted_iota(jnp.int32, p.shape, 0)
    p_up = jnp.where(rows == 0, p_prev, pltpu.roll(p, 1, axis=0))
    p_dn = jnp.where(rows == ROW_TILE - 1, p_next, pltpu.roll(p, ROW_TILE - 1, axis=0))
    cw = cw_ref[...]
    conv = p_up * cw[0:1] + p * cw[1:2] + p_dn * cw[2:3]
    y_a = f32(ab_ref) * conv * _silu(f32(az_ref))
    m = _sigmoid(f32(ga_ref)) * _dot(y_a.astype(BF16), wa_ref[...])

    o = f32(of_ref) + f32(ob_ref)
    gng = gng_ref[...]
    y_b = []
    for h in range(GLA_H):
        sl = slice(h * GLA_DV, (h + 1) * GLA_DV)
        oh = o[:, sl]
        y_b.append(oh * lax.rsqrt(jnp.mean(oh * oh, axis=-1, keepdims=True) + EPS) * gng[:, sl])
    y_b = jnp.concatenate(y_b, axis=-1) * _silu(f32(bz_ref))
    m = m + _sigmoid(f32(gb_ref)) * _dot(y_b.astype(BF16), wb_ref[...])

    y_c = f32(oc_ref) * _silu(f32(cz_ref))
    m = m + _sigmoid(f32(gc_ref)) * _dot(y_c.astype(BF16), wc_ref[...])

    gate = mod_ref[0][:, 2 * D:3 * D]
    out_ref[...] = x_ref[...] + gate * _dot(m.astype(BF16), wo_ref[...])


def _merge(u, o_f, o_b, o_c, xs, mod_l, conv_w, gla_norm_g, wa, wb, wc, wo, layer, last):
    halo_blocks = ROW_TILE // HALO
    n_halo = N_ROWS // HALO
    if last:
        grid = (BATCH * X_TILES,)
        tile = lambda i: (i // X_TILES) * TILES_PER_BATCH + i % X_TILES
        out_rows = BATCH * SEQ
        out_map = lambda i: (i, 0)
    else:
        grid = (N_TILES,)
        tile = lambda i: i
        out_rows = N_ROWS
        out_map = lambda i: (i, 0)

    def col(c):
        return pl.BlockSpec((ROW_TILE, D), lambda i: (tile(i), c))

    def prev_rows(c):
        return pl.BlockSpec((HALO, D), lambda i: (jnp.maximum(tile(i) * halo_blocks - 1, 0), c))

    def next_rows(c):
        return pl.BlockSpec((HALO, D), lambda i: (jnp.minimum((tile(i) + 1) * halo_blocks, n_halo - 1), c))

    row_block = pl.BlockSpec((ROW_TILE, D), lambda i: (tile(i), 0))
    const = lambda i: (0, 0)
    weight = pl.BlockSpec((None, D, D), lambda i: (layer, 0, 0))

    return pl.pallas_call(
        functools.partial(_merge_kernel, tiles_per_batch=X_TILES if last else TILES_PER_BATCH),
        grid=grid,
        in_specs=[col(COL_AV), col(COL_AB), col(COL_AC), col(COL_AZ), col(COL_BZ), col(COL_CZ),
                  col(COL_GA), col(COL_GB), col(COL_GC),
                  prev_rows(COL_AV), prev_rows(COL_AC), next_rows(COL_AV), next_rows(COL_AC),
                  row_block, row_block, row_block, row_block,
                  pl.BlockSpec((1, 1, 3 * D), lambda i: (_tile_group(tile(i)), 0, 0)),
                  pl.BlockSpec((3, D), const), pl.BlockSpec((1, D), const),
                  weight, weight, weight, weight],
        out_specs=pl.BlockSpec((ROW_TILE, D), out_map),
        out_shape=jax.ShapeDtypeStruct((out_rows, D), F32),
        compiler_params=pltpu.CompilerParams(
            dimension_semantics=("arbitrary",), vmem_limit_bytes=VMEM_LIMIT),
        name="merge_last" if last else "merge",
    )(u, u, u, u, u, u, u, u, u, u, u, u, u, o_f, o_b, o_c, xs, mod_l, conv_w, gla_norm_g,
      wa, wb, wc, wo)


def _rope_tables():
    t = np.arange(SEQ)
    row = (t // GRID_W).astype(np.float32)
    colp = (t % GRID_W).astype(np.float32)
    n_freq = MLA_ROPE // 4
    freqs = (np.float32(ROPE_BASE) ** (-np.arange(n_freq, dtype=np.float32) / np.float32(n_freq))).astype(np.float32)
    ang_r = row[:, None] * freqs[None, :]
    ang_c = colp[:, None] * freqs[None, :]
    ang = np.concatenate([ang_r, ang_r, ang_c, ang_c], axis=-1).astype(np.float32)
    cos = np.ones((SB, 128), np.float32)
    sin = np.zeros((SB, 128), np.float32)
    cos[:SEQ, :MLA_ROPE] = np.cos(ang)
    sign = np.where((np.arange(MLA_ROPE) // n_freq) % 2 == 0, -1.0, 1.0).astype(np.float32)
    sin[:SEQ, :MLA_ROPE] = np.sin(ang) * sign[None, :]
    qtab = np.concatenate([cos[:, :MLA_ROPE], sin[:, :MLA_ROPE]], axis=1)
    return jnp.asarray(cos), jnp.asarray(sin), jnp.asarray(qtab)


def _swap16_index():
    l = np.arange(MLA_ROPE)
    return np.where((l // 16) % 2 == 0, l + 16, l - 16)


WP = 1024
W_SPLIT = 7776 - IN_HEAD_COLS
W_SMALL = 7200 - IN_HEAD_COLS
N_WBLOCKS = IN_PAD // WP
HEAD_WBLOCKS = IN_HEAD_COLS // WP


def _permute_w_in_kernel(a_ref, b_ref, o_ref):
    blk = pl.program_id(1)

    @pl.when(blk < HEAD_WBLOCKS)
    def _():
        o_ref[0] = a_ref[0].astype(BF16)

    @pl.when(jnp.logical_and(blk >= HEAD_WBLOCKS, blk < N_WBLOCKS - 1))
    def _():
        o_ref[0, 0:WP - W_SPLIT, :] = a_ref[0, W_SPLIT:WP, :].astype(BF16)
        o_ref[0, WP - W_SPLIT:WP, :] = b_ref[0, 0:W_SPLIT, :].astype(BF16)

    @pl.when(blk == N_WBLOCKS - 1)
    def _():
        o_ref[0, 0:W_SPLIT - W_SMALL, :] = a_ref[0, W_SMALL:W_SPLIT, :].astype(BF16)
        o_ref[0, W_SPLIT - W_SMALL:W_SPLIT, :] = a_ref[0, 0:W_SMALL, :].astype(BF16)
        o_ref[0, W_SPLIT:WP, :] = jnp.zeros((WP - W_SPLIT, D), BF16)


def _permute_w_in(w_in):
    wt = jnp.transpose(w_in, (0, 2, 1))
    last = N_WBLOCKS - 1
    a_map = lambda l, b: (l, jnp.where(b == last, HEAD_WBLOCKS, b), 0)
    b_map = lambda l, b: (l, jnp.clip(b + 1, HEAD_WBLOCKS, last), 0)
    return pl.pallas_call(
        _permute_w_in_kernel,
        grid=(DEPTH, N_WBLOCKS),
        in_specs=[pl.BlockSpec((1, WP, D), a_map), pl.BlockSpec((1, WP, D), b_map)],
        out_specs=pl.BlockSpec((1, WP, D), lambda l, b: (l, b, 0)),
        out_shape=jax.ShapeDtypeStruct((DEPTH, IN_PAD, D), BF16),
        compiler_params=pltpu.CompilerParams(
            dimension_semantics=("arbitrary", "arbitrary"), vmem_limit_bytes=VMEM_LIMIT),
        name="permute_w_in",
    )(wt, wt)


def kernel(x, c, ctx, c_ctx, w_mod, b_mod, norm_g, w_in, conv_w, gla_wa_up_f, gla_ba_f, gla_wa_up_b,
           gla_ba_b, gla_norm_g, mla_q_norm_g, mla_kv_norm_g, mla_wq_up, mla_wkv_up, mla_qn_g, mla_kn_g,
           w_br_a, w_br_b, w_br_c, w_out):
    cos_t, sin_t, qtab_t = _rope_tables()
    perm = _swap16_index()
    xs = jnp.concatenate([x, ctx], axis=1).reshape(N_ROWS, D)
    cc = jnp.concatenate([c, c_ctx[None, :], jnp.zeros((8 - BATCH - 1, D), F32)], axis=0)
    mods = _modulation(cc, w_mod, b_mod).reshape(DEPTH, 8, 1, 3 * D)

    hk = GLA_H * GLA_DK
    w_in_p = _permute_w_in(w_in)
    wup_p = jnp.zeros((DEPTH, 128, 2 * hk), F32)
    wup_p = wup_p.at[:, SMALL_AF:SMALL_AF + GLA_RANK, :hk].set(gla_wa_up_f)
    wup_p = wup_p.at[:, SMALL_AB:SMALL_AB + GLA_RANK, hk:].set(gla_wa_up_b).astype(BF16)
    bup_p = jnp.concatenate([gla_ba_f, gla_ba_b], axis=1)[:, None, :]
    wq4 = mla_wq_up.reshape(DEPTH, MLA_Q_RANK, MLA_H, MLA_QK)
    wq_p = jnp.concatenate([wq4, wq4[:, :, :, MLA_NOPE:][:, :, :, perm]], axis=3)
    wq_p = wq_p.reshape(DEPTH, MLA_Q_RANK, MLA_H * MLA_QPAD).astype(BF16)
    qn_p = jnp.concatenate([mla_qn_g, mla_qn_g[:, MLA_NOPE:][:, perm]], axis=1)[:, None, :]
    kn_p = jnp.pad(mla_kn_g, ((0, 0), (0, MLA_QPAD - MLA_QK)))[:, None, :]
    wkv4 = mla_wkv_up.reshape(DEPTH, MLA_KV_RANK, MLA_H, MLA_NOPE + MLA_DV)
    wk = wkv4[:, :, :, :MLA_NOPE].reshape(DEPTH, MLA_KV_RANK, MLA_H * MLA_NOPE).astype(BF16)
    wvt = wkv4[:, :, :, MLA_NOPE:].reshape(DEPTH, MLA_KV_RANK, MLA_H * MLA_DV).transpose(0, 2, 1).astype(BF16)
    wa, wb, wc, wo = (w.astype(BF16) for w in (w_br_a, w_br_b, w_br_c, w_out))

    for l in range(DEPTH):
        last = l == DEPTH - 1
        u = _inproj(xs, mods[l], norm_g[l][None, :], w_in_p, l)
        o_f, o_b, q, k, vt = _gla_and_mla_prep(
            u, wup_p[l], bup_p[l], cos_t, sin_t, qtab_t, wq_p[l], wk[l], wvt[l],
            mla_q_norm_g[l][None, :], mla_kv_norm_g[l][None, :], qn_p[l], kn_p[l])
        o_c = _attention(q, k, vt)
        xs = _merge(u, o_f, o_b, o_c, xs, mods[l], conv_w[l], gla_norm_g[l][None, :],
                    wa, wb, wc, wo, l, last)
    return xs.reshape(BATCH, SEQ, D)
```

```python
import functools
import math

import numpy as np
import jax
import jax.numpy as jnp
from jax import lax
from jax.experimental import pallas as pl
from jax.experimental.pallas import tpu as pltpu

D = 1024
BATCH = 4
SEQ = 4096
DEPTH = 4
GRID_W = 64
CTX = 256
EPS = 1e-6

GLA_H = 4
GLA_DK = 128
GLA_DV = 256
GLA_RANK = 16
GLA_TAU = 16.0
CHUNK = 64

MLA_H = 8
MLA_NOPE = 128
MLA_ROPE = 64
MLA_DV = 128
MLA_QK = MLA_NOPE + MLA_ROPE
MLA_QPAD = 256
VT_ROWS = MLA_DV + 16
_SM_SCALE_LOG2E = (MLA_QK ** -0.5) * math.log2(math.e)
MLA_Q_RANK = 384
MLA_KV_RANK = 128
ROPE_BASE = 10000.0

SB = SEQ + CTX
N_ROWS = BATCH * SB
ROW_TILE = 256
TILES_PER_BATCH = SB // ROW_TILE
X_TILES = SEQ // ROW_TILE
N_TILES = N_ROWS // ROW_TILE
CTX_GROUP = BATCH

IN_PAD = 12288
IN_HEAD_COLS = 7168
COL_AV, COL_AB, COL_AC, COL_AZ, _, COL_BV, COL_BZ, COL_CZ, COL_GA, COL_GB, COL_GC = range(11)
COL_BQ, COL_BK, COL_CQKV = 8, 9, 22
COL_SMALL = 92
SMALL_KR, SMALL_AF, SMALL_AB = 0, 64, 80

IN_TN = 2048
IN_PREP_ROWS = 272
IN_TILES_PER_BATCH = 4
IN_TM = SB // IN_TILES_PER_BATCH
Q_TILE = 1024
KV_TILE = 512
ATTN_LOOKAHEAD = 2

VMEM_LIMIT = 56 * 1024 * 1024

F32 = jnp.float32
BF16 = jnp.bfloat16
ACT = BF16
HALO = 16


def _dot(a, b):
    return jnp.dot(a, b, preferred_element_type=F32)


def _dot_nt(a, b):
    return lax.dot_general(a, b, (((1,), (1,)), ((), ())), preferred_element_type=F32)


def _dot_exact(a, b):
    return jnp.dot(a, b, preferred_element_type=F32, precision=lax.Precision.HIGHEST)


def _sigmoid(x):
    return 1.0 / (1.0 + jnp.exp2(x * -math.log2(math.e)))


def _silu(x):
    return x * _sigmoid(x)


def _tile_group(i):
    tt = i % TILES_PER_BATCH
    return jnp.where(tt >= X_TILES, CTX_GROUP, i // TILES_PER_BATCH)


def _mod_kernel(cc_ref, w_ref, b_ref, o_ref):
    o_ref[0] = _dot_exact(_silu(cc_ref[...]), w_ref[0]) + b_ref[0]


def _modulation(cc, w_mod, b_mod):
    return pl.pallas_call(
        _mod_kernel,
        grid=(DEPTH,),
        in_specs=[
            pl.BlockSpec((8, D), lambda l: (0, 0)),
            pl.BlockSpec((1, D, 3 * D), lambda l: (l, 0, 0)),
            pl.BlockSpec((1, 1, 3 * D), lambda l: (l, 0, 0)),
        ],
        out_specs=pl.BlockSpec((1, 8, 3 * D), lambda l: (l, 0, 0)),
        out_shape=jax.ShapeDtypeStruct((DEPTH, 8, 3 * D), F32),
        compiler_params=pltpu.CompilerParams(
            dimension_semantics=("arbitrary",), vmem_limit_bytes=VMEM_LIMIT),
        name="modulation",
    )(cc, w_mod, b_mod.reshape(DEPTH, 1, 3 * D))


def _inproj_kernel(x_ref, modx_ref, modc_ref, g_ref, w_ref, u_ref, h_ref):
    @pl.when(pl.program_id(1) == 0)
    def _():
        modx = modx_ref[0]
        modc = modc_ref[0]
        for r in range(0, IN_TM, IN_PREP_ROWS):
            x = x_ref[r:r + IN_PREP_ROWS, :]
            y = x * lax.rsqrt(jnp.mean(x * x, axis=-1, keepdims=True) + EPS) * g_ref[...]
            row = ((pl.program_id(0) % IN_TILES_PER_BATCH) * IN_TM + r
                   + lax.broadcasted_iota(jnp.int32, x.shape, 0))
            is_ctx = row >= SEQ
            scale = jnp.where(is_ctx, modc[:, D:2 * D], modx[:, D:2 * D])
            shift = jnp.where(is_ctx, modc[:, 0:D], modx[:, 0:D])
            h = (y * (1.0 + scale) + shift).astype(BF16)
            h_ref[r:r + IN_PREP_ROWS, :] = h
            u_ref[r:r + IN_PREP_ROWS, :] = _dot_nt(h, w_ref[...]).astype(u_ref.dtype)

    @pl.when(pl.program_id(1) > 0)
    def _():
        u_ref[...] = _dot_nt(h_ref[...], w_ref[...]).astype(u_ref.dtype)


def _inproj(xs, mod_l, norm_g, w_in_p, layer):
    return pl.pallas_call(
        _inproj_kernel,
        grid=(N_ROWS // IN_TM, IN_PAD // IN_TN),
        in_specs=[
            pl.BlockSpec((IN_TM, D), lambda i, j: (i, 0)),
            pl.BlockSpec((1, 1, 3 * D), lambda i, j: (i // IN_TILES_PER_BATCH, 0, 0)),
            pl.BlockSpec((1, 1, 3 * D), lambda i, j: (CTX_GROUP, 0, 0)),
            pl.BlockSpec((1, D), lambda i, j: (0, 0)),
            pl.BlockSpec((None, IN_TN, D), lambda i, j: (layer, j, 0)),
        ],
        out_specs=pl.BlockSpec((IN_TM, IN_TN), lambda i, j: (i, j)),
        out_shape=jax.ShapeDtypeStruct((N_ROWS, IN_PAD), ACT),
        scratch_shapes=[pltpu.VMEM((IN_TM, D), BF16)],
        compiler_params=pltpu.CompilerParams(
            dimension_semantics=("arbitrary", "arbitrary"), vmem_limit_bytes=VMEM_LIMIT),
        name="inproj",
    )(xs, mod_l, mod_l, norm_g, w_in_p)


def _swap16(r):
    lane = lax.broadcasted_iota(jnp.int32, r.shape, 1)
    even = ((lane // 16) % 2) == 0
    return jnp.where(even, pltpu.roll(r, 112, axis=1), pltpu.roll(r, 16, axis=1))


def _mla_project(cqkv_ref, wq_ref, wk_ref, wvt_ref, gq_ref, gkv_ref):
    cqkv = cqkv_ref[...].astype(F32)
    cq = cqkv[:, :MLA_Q_RANK]
    ckv = cqkv[:, MLA_Q_RANK:]

    cq_n = cq * lax.rsqrt(jnp.mean(cq * cq, axis=-1, keepdims=True) + EPS) * gq_ref[...]
    q_all = _dot(cq_n.astype(BF16), wq_ref[...])
    ckv_n = (ckv * lax.rsqrt(jnp.mean(ckv * ckv, axis=-1, keepdims=True) + EPS) * gkv_ref[...]).astype(BF16)
    kn_all = _dot(ckv_n, wk_ref[...])
    vt_all = _dot_nt(wvt_ref[...], ckv_n)
    return q_all, kn_all, vt_all


def _mla_finish(q_all, kn_all, vt_all, small_ref, cos_ref, sin_ref, qtab_ref, qn_ref, kn_ref, q_ref, k_ref, vt_ref):
    small = small_ref[...].astype(F32)
    lane = lax.broadcasted_iota(jnp.int32, small.shape, 1)
    kr = jnp.where(lane < MLA_ROPE, small, 0.0)
    kr_ss = jnp.sum(kr * kr, axis=-1, keepdims=True)
    qn_g = qn_ref[...] * _SM_SCALE_LOG2E
    kn_g = kn_ref[...]
    krg = kr * kn_g[:, MLA_NOPE:MLA_QPAD]
    k_rot = krg * cos_ref[...] + _swap16(krg) * sin_ref[...]
    k_rot2 = k_rot + pltpu.roll(k_rot, MLA_ROPE, axis=1)
    qtab = qtab_ref[...] * qn_g[:, MLA_NOPE:MLA_QPAD]
    ones_rows = jnp.ones((VT_ROWS - MLA_DV, ROW_TILE), BF16)

    for h in range(MLA_H):
        q0 = q_all[:, h * MLA_QPAD:h * MLA_QPAD + MLA_NOPE]
        q1 = q_all[:, h * MLA_QPAD + MLA_NOPE:(h + 1) * MLA_QPAD]
        ss = jnp.sum(q0 * q0 + 0.5 * (q1 * q1), axis=-1, keepdims=True)
        inv = lax.rsqrt(ss * (1.0 / MLA_QK) + EPS)
        q_ref[h, :, 0:MLA_NOPE] = (q0 * inv * qn_g[:, 0:MLA_NOPE]).astype(BF16)
        q_ref[h, :, MLA_NOPE:MLA_QPAD] = (q1 * inv * qtab).astype(BF16)

        kn = kn_all[:, h * MLA_NOPE:(h + 1) * MLA_NOPE]
        inv = lax.rsqrt((jnp.sum(kn * kn, axis=-1, keepdims=True) + kr_ss) * (1.0 / MLA_QK) + EPS)
        k_ref[h, :, 0:MLA_NOPE] = (kn * inv * kn_g[:, 0:MLA_NOPE]).astype(BF16)
        k_ref[h, :, MLA_NOPE:MLA_QPAD] = (k_rot2 * inv).astype(BF16)
        vt_ref[h, 0:MLA_DV, :] = vt_all[h * MLA_DV:(h + 1) * MLA_DV, :].astype(BF16)
        vt_ref[h, MLA_DV:VT_ROWS, :] = ones_rows


def _attn_kernel(q_ref, k_ref, vt_ref, o_ref):
    n_kv = SEQ // KV_TILE
    lat_bounds = [(c * KV_TILE, (c + 1) * KV_TILE) for c in range(n_kv - 1)] + [((n_kv - 1) * KV_TILE, SB)]
    tiles = [((r0, r0 + Q_TILE), lat_bounds) for r0 in range(0, SEQ, Q_TILE)] + [((SEQ, SB), [(SEQ, SB)])]
    items = [(t, c) for t, (_, bounds) in enumerate(tiles) for c in range(len(bounds))]

    def scores(n):
        t, c = items[n]
        (q0, q1), bounds = tiles[t]
        lo, hi = bounds[c]
        return _dot_nt(k_ref[0, lo:hi, :], q_ref[0, q0:q1, :])

    ahead = [scores(n) for n in range(ATTN_LOOKAHEAD)]
    m = acc = None
    for n, (t, c) in enumerate(items):
        s = ahead.pop(0)
        if n + ATTN_LOOKAHEAD < len(items):
            ahead.append(scores(n + ATTN_LOOKAHEAD))
        (q0, q1), bounds = tiles[t]
        lo, hi = bounds[c]
        m_c = jnp.max(s, axis=0, keepdims=True)
        m_new = m_c if c == 0 else jnp.maximum(m, m_c)
        p = jnp.exp2(s - m_new)
        pv = _dot(vt_ref[0, :, lo:hi], p.astype(BF16))
        acc = pv if c == 0 else jnp.exp2(m - m_new) * acc + pv
        m = m_new
        if c == len(bounds) - 1:
            o_ref[q0:q1, :] = (acc[0:MLA_DV] / acc[MLA_DV:MLA_DV + 1]).T.astype(o_ref.dtype)


def _attention(q, k, vt):
    return pl.pallas_call(
        _attn_kernel,
        grid=(BATCH, MLA_H),
        in_specs=[
            pl.BlockSpec((1, SB, MLA_QPAD), lambda b, h: (h, b, 0)),
            pl.BlockSpec((1, SB, MLA_QPAD), lambda b, h: (h, b, 0)),
            pl.BlockSpec((1, VT_ROWS, SB), lambda b, h: (h, 0, b)),
        ],
        out_specs=pl.BlockSpec((SB, MLA_DV), lambda b, h: (b, h)),
        out_shape=jax.ShapeDtypeStruct((N_ROWS, MLA_H * MLA_DV), ACT),
        compiler_params=pltpu.CompilerParams(
            dimension_semantics=("arbitrary", "arbitrary"), vmem_limit_bytes=VMEM_LIMIT),
        name="mla_attention",
    )(q, k, vt)


def _log2_sigmoid(z):
    z2 = z * math.log2(math.e)
    return jnp.minimum(z2, 0.0) - jnp.log2(1.0 + jnp.exp2(-jnp.abs(z2)))


def _gla_kernel(qf_ref, kf_ref, vf_ref, sf_ref, qb_ref, kb_ref, vb_ref, sb_ref, wup_ref, bup_ref,
                cqkv_ref, cos_ref, sin_ref, qtab_ref, wq_ref, wk_ref, wvt_ref, gq_ref, gkv_ref, qn_ref, kn_ref,
                of_ref, ob_ref, mq_ref, mk_ref, mvt_ref, st_ref):
    @pl.when(pl.program_id(1) == 0)
    def _():
        st_ref[...] = jnp.zeros_like(st_ref)

    mla = _mla_project(cqkv_ref, wq_ref, wk_ref, wvt_ref, gq_ref, gkv_ref)

    hk = GLA_H * GLA_DK
    n_chunks = ROW_TILE // CHUNK
    row = lax.broadcasted_iota(jnp.int32, (ROW_TILE, ROW_TILE), 0)
    col = lax.broadcasted_iota(jnp.int32, (ROW_TILE, ROW_TILE), 1)
    same_chunk = (row // CHUNK) == (col // CHUNK)
    q_scale = GLA_DK ** -0.5

    dirs = ((qf_ref, kf_ref, vf_ref, sf_ref, of_ref), (qb_ref, kb_ref, vb_ref, sb_ref, ob_ref))
    heads = [(d, h) for d in range(2) for h in range(GLA_H)]
    ksl = lambda h: slice(h * GLA_DK, (h + 1) * GLA_DK)
    vsl = lambda h: slice(h * GLA_DV, (h + 1) * GLA_DV)
    csl = lambda c: slice(c * CHUNK, (c + 1) * CHUNK)
    keep = [jnp.logical_and(same_chunk, row >= col), jnp.logical_and(same_chunk, row <= col)]
    scan = [list(range(n_chunks)), list(range(n_chunks - 1, -1, -1))]

    z = [_dot(dirs[d][3][...], wup_ref[:, d * hk:(d + 1) * hk]) + bup_ref[:, d * hk:(d + 1) * hk] for d in range(2)]
    cs = []
    for d in range(2):
        la = _log2_sigmoid(z[d]) * (1.0 / GLA_TAU)
        la_hi = la.astype(BF16)
        la_lo = (la - la_hi.astype(F32)).astype(BF16)
        tri = jnp.where(keep[d], 1.0, 0.0).astype(BF16)
        cs.append(_dot(tri, jnp.concatenate([la_hi, la_lo], axis=1)))

    q_dec, k_inv, k_end, g = [], [], [], []
    for d in range(2):
        q_ref, k_ref = dirs[d][0], dirs[d][1]
        b = cs[d][:, :hk] + cs[d][:, hk:]
        edge = (lambda c: c * CHUNK + CHUNK - 1) if d == 0 else (lambda c: c * CHUNK)
        tots = [b[edge(c):edge(c) + 1] for c in range(n_chunks)]
        tot = jnp.concatenate([jnp.broadcast_to(t, (CHUNK, hk)) for t in tots], axis=0)
        kf = k_ref[...].astype(F32)
        q_dec.append(((q_ref[...].astype(F32) * q_scale) * jnp.exp2(b)).astype(BF16))
        k_inv.append((kf * jnp.exp2(-b)).astype(BF16))
        k_end.append(kf * jnp.exp2(tot - b))
        g.append([jnp.exp2(t) for t in tots])

    att = {(d, h): _dot_nt(q_dec[d][:, ksl(h)], k_inv[d][:, ksl(h)]) for (d, h) in heads}
    upd = {(d, h, c): _dot(k_end[d][csl(c), ksl(h)].T.astype(BF16), dirs[d][2][csl(c), vsl(h)])
           for (d, h) in heads for c in range(n_chunks)}
    o_intra = {(d, h): _dot(jnp.where(keep[d], att[d, h], 0.0).astype(BF16), dirs[d][2][:, vsl(h)])
               for (d, h) in heads}

    _mla_finish(*mla, sf_ref, cos_ref, sin_ref, qtab_ref, qn_ref, kn_ref, mq_ref, mk_ref, mvt_ref)

    states = {}
    for (d, h) in heads:
        st = st_ref[d, h]
        for c in scan[d]:
            states[d, h, c] = st.astype(BF16)
            g_col = jnp.broadcast_to(g[d][c][:, ksl(h)], (GLA_DK, GLA_DK)).T
            st = st * jnp.concatenate([g_col] * (GLA_DV // GLA_DK), axis=1) + upd[d, h, c]
        st_ref[d, h] = st

    for (d, h) in heads:
        o_ref = dirs[d][4]
        for c in range(n_chunks):
            o_inter = _dot(q_dec[d][csl(c), ksl(h)], states[d, h, c])
            o_ref[csl(c), vsl(h)] = (o_intra[d, h][csl(c)] + o_inter).astype(o_ref.dtype)


def _gla_and_mla_prep(u, wup_p, bup_p, cos_t, sin_t, qtab_t, wq_p, wk, wvt, gq, gkv, qn_p, kn_p):
    def fwd_pos(s):
        return jnp.where(s == 0, X_TILES, s - 1)

    def fwd_tile(b, s):
        return b * TILES_PER_BATCH + fwd_pos(s)

    def bwd_tile(b, s):
        return b * TILES_PER_BATCH + jnp.where(s == 0, X_TILES, X_TILES - s)

    def specs(tile):
        return [
            pl.BlockSpec((ROW_TILE, 512), lambda b, s: (tile(b, s), COL_BQ)),
            pl.BlockSpec((ROW_TILE, 512), lambda b, s: (tile(b, s), COL_BK)),
            pl.BlockSpec((ROW_TILE, 1024), lambda b, s: (tile(b, s), COL_BV)),
            pl.BlockSpec((ROW_TILE, 128), lambda b, s: (tile(b, s), COL_SMALL)),
        ]

    const = lambda b, s: (0, 0)
    table = pl.BlockSpec((ROW_TILE, 128), lambda b, s: (fwd_pos(s), 0))
    return pl.pallas_call(
        _gla_kernel,
        grid=(BATCH, TILES_PER_BATCH),
        in_specs=specs(fwd_tile) + specs(bwd_tile) + [
            pl.BlockSpec((128, 2 * GLA_H * GLA_DK), const),
            pl.BlockSpec((1, 2 * GLA_H * GLA_DK), const),
            pl.BlockSpec((ROW_TILE, 512), lambda b, s: (fwd_tile(b, s), COL_CQKV)),
            table, table, table,
            pl.BlockSpec((MLA_Q_RANK, MLA_H * MLA_QPAD), const),
            pl.BlockSpec((MLA_KV_RANK, MLA_H * MLA_NOPE), const),
            pl.BlockSpec((MLA_H * MLA_DV, MLA_KV_RANK), const),
            pl.BlockSpec((1, MLA_Q_RANK), const),
            pl.BlockSpec((1, MLA_KV_RANK), const),
            pl.BlockSpec((1, MLA_QPAD), const),
            pl.BlockSpec((1, MLA_QPAD), const),
        ],
        out_specs=[
            pl.BlockSpec((ROW_TILE, GLA_H * GLA_DV), lambda b, s: (fwd_tile(b, s), 0)),
            pl.BlockSpec((ROW_TILE, GLA_H * GLA_DV), lambda b, s: (bwd_tile(b, s), 0)),
            pl.BlockSpec((MLA_H, ROW_TILE, MLA_QPAD), lambda b, s: (0, fwd_tile(b, s), 0)),
            pl.BlockSpec((MLA_H, ROW_TILE, MLA_QPAD), lambda b, s: (0, fwd_tile(b, s), 0)),
            pl.BlockSpec((MLA_H, VT_ROWS, ROW_TILE), lambda b, s: (0, 0, fwd_tile(b, s))),
        ],
        out_shape=[
            jax.ShapeDtypeStruct((N_ROWS, GLA_H * GLA_DV), ACT),
            jax.ShapeDtypeStruct((N_ROWS, GLA_H * GLA_DV), ACT),
            jax.ShapeDtypeStruct((MLA_H, N_ROWS, MLA_QPAD), BF16),
            jax.ShapeDtypeStruct((MLA_H, N_ROWS, MLA_QPAD), BF16),
            jax.ShapeDtypeStruct((MLA_H, VT_ROWS, N_ROWS), BF16),
        ],
        scratch_shapes=[pltpu.VMEM((2, GLA_H, GLA_DK, GLA_DV), F32)],
        compiler_params=pltpu.CompilerParams(
            dimension_semantics=("arbitrary", "arbitrary"), vmem_limit_bytes=VMEM_LIMIT),
        name="gla_bidir_mla_prep",
    )(u, u, u, u, u, u, u, u, wup_p, bup_p, u, cos_t, sin_t, qtab_t, wq_p, wk, wvt, gq, gkv, qn_p, kn_p)


def _merge_kernel(av_ref, ab_ref, ac_ref, az_ref, bz_ref, cz_ref, ga_ref, gb_ref, gc_ref,
                  avp_ref, acp_ref, avn_ref, acn_ref, of_ref, ob_ref, oc_ref, x_ref, mod_ref,
                  cw_ref, gng_ref, wa_ref, wb_ref, wc_ref, wo_ref, out_ref, *, tiles_per_batch):
    tt = pl.program_id(0) % tiles_per_batch
    prev_ok = jnp.logical_and(tt != 0, tt < X_TILES).astype(F32)
    next_ok = (tt < X_TILES - 1).astype(F32)

    f32 = lambda ref: ref[...].astype(F32)
    p = f32(ac_ref) * f32(av_ref)
    p_prev = f32(acp_ref)[HALO - 1:HALO, :] * f32(avp_ref)[HALO - 1:HALO, :] * prev_ok
    p_next = f32(acn_ref)[0:1, :] * f32(avn_ref)[0:1, :] * next_ok
    rows = lax.broadcasted_iota(jnp.int32, p.shape, 0)
    p_up = jnp.where(rows == 0, p_prev, pltpu.roll(p, 1, axis=0))
    p_dn = jnp.where(rows == ROW_TILE - 1, p_next, pltpu.roll(p, ROW_TILE - 1, axis=0))
    cw = cw_ref[...]
    conv = p_up * cw[0:1] + p * cw[1:2] + p_dn * cw[2:3]
    y_a = f32(ab_ref) * conv * _silu(f32(az_ref))
    m = _sigmoid(f32(ga_ref)) * _dot(y_a.astype(BF16), wa_ref[...])

    o = f32(of_ref) + f32(ob_ref)
    gng = gng_ref[...]
    y_b = []
    for h in range(GLA_H):
        sl = slice(h * GLA_DV, (h + 1) * GLA_DV)
        oh = o[:, sl]
        y_b.append(oh * lax.rsqrt(jnp.mean(oh * oh, axis=-1, keepdims=True) + EPS) * gng[:, sl])
    y_b = jnp.concatenate(y_b, axis=-1) * _silu(f32(bz_ref))
    m = m + _sigmoid(f32(gb_ref)) * _dot(y_b.astype(BF16), wb_ref[...])

    y_c = f32(oc_ref) * _silu(f32(cz_ref))
    m = m + _sigmoid(f32(gc_ref)) * _dot(y_c.astype(BF16), wc_ref[...])

    gate = mod_ref[0][:, 2 * D:3 * D]
    out_ref[...] = x_ref[...] + gate * _dot(m.astype(BF16), wo_ref[...])


def _merge(u, o_f, o_b, o_c, xs, mod_l, conv_w, gla_norm_g, wa, wb, wc, wo, layer, last):
    halo_blocks = ROW_TILE // HALO
    n_halo = N_ROWS // HALO
    if last:
        grid = (BATCH * X_TILES,)
        tile = lambda i: (i // X_TILES) * TILES_PER_BATCH + i % X_TILES
        out_rows = BATCH * SEQ
        out_map = lambda i: (i, 0)
    else:
        grid = (N_TILES,)
        tile = lambda i: i
        out_rows = N_ROWS
        out_map = lambda i: (i, 0)

    def col(c):
        return pl.BlockSpec((ROW_TILE, D), lambda i: (tile(i), c))

    def prev_rows(c):
        return pl.BlockSpec((HALO, D), lambda i: (jnp.maximum(tile(i) * halo_blocks - 1, 0), c))

    def next_rows(c):
        return pl.BlockSpec((HALO, D), lambda i: (jnp.minimum((tile(i) + 1) * halo_blocks, n_halo - 1), c))

    row_block = pl.BlockSpec((ROW_TILE, D), lambda i: (tile(i), 0))
    const = lambda i: (0, 0)
    weight = pl.BlockSpec((None, D, D), lambda i: (layer, 0, 0))

    return pl.pallas_call(
        functools.partial(_merge_kernel, tiles_per_batch=X_TILES if last else TILES_PER_BATCH),
        grid=grid,
        in_specs=[col(COL_AV), col(COL_AB), col(COL_AC), col(COL_AZ), col(COL_BZ), col(COL_CZ),
                  col(COL_GA), col(COL_GB), col(COL_GC),
                  prev_rows(COL_AV), prev_rows(COL_AC), next_rows(COL_AV), next_rows(COL_AC),
                  row_block, row_block, row_block, row_block,
                  pl.BlockSpec((1, 1, 3 * D), lambda i: (_tile_group(tile(i)), 0, 0)),
                  pl.BlockSpec((3, D), const), pl.BlockSpec((1, D), const),
                  weight, weight, weight, weight],
        out_specs=pl.BlockSpec((ROW_TILE, D), out_map),
        out_shape=jax.ShapeDtypeStruct((out_rows, D), F32),
        compiler_params=pltpu.CompilerParams(
            dimension_semantics=("arbitrary",), vmem_limit_bytes=VMEM_LIMIT),
        name="merge_last" if last else "merge",
    )(u, u, u, u, u, u, u, u, u, u, u, u, u, o_f, o_b, o_c, xs, mod_l, conv_w, gla_norm_g,
      wa, wb, wc, wo)


def _rope_tables():
    t = np.arange(SEQ)
    row = (t // GRID_W).astype(np.float32)
    colp = (t % GRID_W).astype(np.float32)
    n_freq = MLA_ROPE // 4
    freqs = (np.float32(ROPE_BASE) ** (-np.arange(n_freq, dtype=np.float32) / np.float32(n_freq))).astype(np.float32)
    ang_r = row[:, None] * freqs[None, :]
    ang_c = colp[:, None] * freqs[None, :]
    ang = np.concatenate([ang_r, ang_r, ang_c, ang_c], axis=-1).astype(np.float32)
    cos = np.ones((SB, 128), np.float32)
    sin = np.zeros((SB, 128), np.float32)
    cos[:SEQ, :MLA_ROPE] = np.cos(ang)
    sign = np.where((np.arange(MLA_ROPE) // n_freq) % 2 == 0, -1.0, 1.0).astype(np.float32)
    sin[:SEQ, :MLA_ROPE] = np.sin(ang) * sign[None, :]
    qtab = np.concatenate([cos[:, :MLA_ROPE], sin[:, :MLA_ROPE]], axis=1)
    return jnp.asarray(cos), jnp.asarray(sin), jnp.asarray(qtab)


def _swap16_index():
    l = np.arange(MLA_ROPE)
    return np.where((l // 16) % 2 == 0, l + 16, l - 16)


WP = 1024
W_SPLIT = 7776 - IN_HEAD_COLS
W_SMALL = 7200 - IN_HEAD_COLS
N_WBLOCKS = IN_PAD // WP
HEAD_WBLOCKS = IN_HEAD_COLS // WP


def _permute_w_in_kernel(a_ref, b_ref, o_ref):
    blk = pl.program_id(1)

    @pl.when(blk < HEAD_WBLOCKS)
    def _():
        o_ref[0] = a_ref[0].astype(BF16)

    @pl.when(jnp.logical_and(blk >= HEAD_WBLOCKS, blk < N_WBLOCKS - 1))
    def _():
        o_ref[0, 0:WP - W_SPLIT, :] = a_ref[0, W_SPLIT:WP, :].astype(BF16)
        o_ref[0, WP - W_SPLIT:WP, :] = b_ref[0, 0:W_SPLIT, :].astype(BF16)

    @pl.when(blk == N_WBLOCKS - 1)
    def _():
        o_ref[0, 0:W_SPLIT - W_SMALL, :] = a_ref[0, W_SMALL:W_SPLIT, :].astype(BF16)
        o_ref[0, W_SPLIT - W_SMALL:W_SPLIT, :] = a_ref[0, 0:W_SMALL, :].astype(BF16)
        o_ref[0, W_SPLIT:WP, :] = jnp.zeros((WP - W_SPLIT, D), BF16)


def _permute_w_in(w_in):
    wt = jnp.transpose(w_in, (0, 2, 1))
    last = N_WBLOCKS - 1
    a_map = lambda l, b: (l, jnp.where(b == last, HEAD_WBLOCKS, b), 0)
    b_map = lambda l, b: (l, jnp.clip(b + 1, HEAD_WBLOCKS, last), 0)
    return pl.pallas_call(
        _permute_w_in_kernel,
        grid=(DEPTH, N_WBLOCKS),
        in_specs=[pl.BlockSpec((1, WP, D), a_map), pl.BlockSpec((1, WP, D), b_map)],
        out_specs=pl.BlockSpec((1, WP, D), lambda l, b: (l, b, 0)),
        out_shape=jax.ShapeDtypeStruct((DEPTH, IN_PAD, D), BF16),
        compiler_params=pltpu.CompilerParams(
            dimension_semantics=("arbitrary", "arbitrary"), vmem_limit_bytes=VMEM_LIMIT),
        name="permute_w_in",
    )(wt, wt)


def kernel(x, c, ctx, c_ctx, w_mod, b_mod, norm_g, w_in, conv_w, gla_wa_up_f, gla_ba_f, gla_wa_up_b,
           gla_ba_b, gla_norm_g, mla_q_norm_g, mla_kv_norm_g, mla_wq_up, mla_wkv_up, mla_qn_g, mla_kn_g,
           w_br_a, w_br_b, w_br_c, w_out):
    cos_t, sin_t, qtab_t = _rope_tables()
    perm = _swap16_index()
    xs = jnp.concatenate([x, ctx], axis=1).reshape(N_ROWS, D)
    cc = jnp.concatenate([c, c_ctx[None, :], jnp.zeros((8 - BATCH - 1, D), F32)], axis=0)
    mods = _modulation(cc, w_mod, b_mod).reshape(DEPTH, 8, 1, 3 * D)

    hk = GLA_H * GLA_DK
    w_in_p = _permute_w_in(w_in)
    wup_p = jnp.zeros((DEPTH, 128, 2 * hk), F32)
    wup_p = wup_p.at[:, SMALL_AF:SMALL_AF + GLA_RANK, :hk].set(gla_wa_up_f)
    wup_p = wup_p.at[:, SMALL_AB:SMALL_AB + GLA_RANK, hk:].set(gla_wa_up_b).astype(BF16)
    bup_p = jnp.concatenate([gla_ba_f, gla_ba_b], axis=1)[:, None, :]
    wq4 = mla_wq_up.reshape(DEPTH, MLA_Q_RANK, MLA_H, MLA_QK)
    wq_p = jnp.concatenate([wq4, wq4[:, :, :, MLA_NOPE:][:, :, :, perm]], axis=3)
    wq_p = wq_p.reshape(DEPTH, MLA_Q_RANK, MLA_H * MLA_QPAD).astype(BF16)
    qn_p = jnp.concatenate([mla_qn_g, mla_qn_g[:, MLA_NOPE:][:, perm]], axis=1)[:, None, :]
    kn_p = jnp.pad(mla_kn_g, ((0, 0), (0, MLA_QPAD - MLA_QK)))[:, None, :]
    wkv4 = mla_wkv_up.reshape(DEPTH, MLA_KV_RANK, MLA_H, MLA_NOPE + MLA_DV)
    wk = wkv4[:, :, :, :MLA_NOPE].reshape(DEPTH, MLA_KV_RANK, MLA_H * MLA_NOPE).astype(BF16)
    wvt = wkv4[:, :, :, MLA_NOPE:].reshape(DEPTH, MLA_KV_RANK, MLA_H * MLA_DV).transpose(0, 2, 1).astype(BF16)
    wa, wb, wc, wo = (w.astype(BF16) for w in (w_br_a, w_br_b, w_br_c, w_out))

    for l in range(DEPTH):
        last = l == DEPTH - 1
        u = _inproj(xs, mods[l], norm_g[l][None, :], w_in_p, l)
        o_f, o_b, q, k, vt = _gla_and_mla_prep(
            u, wup_p[l], bup_p[l], cos_t, sin_t, qtab_t, wq_p[l], wk[l], wvt[l],
            mla_q_norm_g[l][None, :], mla_kv_norm_g[l][None, :], qn_p[l], kn_p[l])
        o_c = _attention(q, k, vt)
        xs = _merge(u, o_f, o_b, o_c, xs, mods[l], conv_w[l], gla_norm_g[l][None, :],
                    wa, wb, wc, wo, l, last)
    return xs.reshape(BATCH, SEQ, D)
```

```python
import functools
import math

import numpy as np
import jax
import jax.numpy as jnp
from jax import lax
from jax.experimental import pallas as pl
from jax.experimental.pallas import tpu as pltpu

D = 1024
BATCH = 4
SEQ = 4096
DEPTH = 4
GRID_W = 64
CTX = 256
EPS = 1e-6

GLA_H = 4
GLA_DK = 128
GLA_DV = 256
GLA_RANK = 16
GLA_TAU = 16.0
CHUNK = 64

MLA_H = 8
MLA_NOPE = 128
MLA_ROPE = 64
MLA_DV = 128
MLA_QK = MLA_NOPE + MLA_ROPE
MLA_QPAD = 256
VT_ROWS = MLA_DV + 16
_SM_SCALE_LOG2E = (MLA_QK ** -0.5) * math.log2(math.e)
MLA_Q_RANK = 384
MLA_KV_RANK = 128
ROPE_BASE = 10000.0

SB = SEQ + CTX
N_ROWS = BATCH * SB
ROW_TILE = 256
TILES_PER_BATCH = SB // ROW_TILE
X_TILES = SEQ // ROW_TILE
N_TILES = N_ROWS // ROW_TILE
CTX_GROUP = BATCH

IN_PAD = 12288
IN_HEAD_COLS = 7168
COL_AV, COL_AB, COL_AC, COL_AZ, _, COL_BV, COL_BZ, COL_CZ, COL_GA, COL_GB, COL_GC = range(11)
COL_BQ, COL_BK, COL_CQKV = 8, 9, 22
COL_SMALL = 92
SMALL_KR, SMALL_AF, SMALL_AB = 0, 64, 80

IN_TN = 2048
IN_PREP_ROWS = 272
IN_TILES_PER_BATCH = 4
IN_TM = SB // IN_TILES_PER_BATCH
Q_TILE = 1024
KV_TILE = 512
ATTN_LOOKAHEAD = 2

VMEM_LIMIT = 56 * 1024 * 1024

F32 = jnp.float32
BF16 = jnp.bfloat16
ACT = BF16
HALO = 16


def _dot(a, b):
    return jnp.dot(a, b, preferred_element_type=F32)


def _dot_nt(a, b):
    return lax.dot_general(a, b, (((1,), (1,)), ((), ())), preferred_element_type=F32)


def _dot_exact(a, b):
    return jnp.dot(a, b, preferred_element_type=F32, precision=lax.Precision.HIGHEST)


def _sigmoid(x):
    return 1.0 / (1.0 + jnp.exp2(x * -math.log2(math.e)))


def _silu(x):
    return x * _sigmoid(x)


def _tile_group(i):
    tt = i % TILES_PER_BATCH
    return jnp.where(tt >= X_TILES, CTX_GROUP, i // TILES_PER_BATCH)


def _mod_kernel(cc_ref, w_ref, b_ref, o_ref):
    o_ref[0] = _dot_exact(_silu(cc_ref[...]), w_ref[0]) + b_ref[0]


def _modulation(cc, w_mod, b_mod):
    return pl.pallas_call(
        _mod_kernel,
        grid=(DEPTH,),
        in_specs=[
            pl.BlockSpec((8, D), lambda l: (0, 0)),
            pl.BlockSpec((1, D, 3 * D), lambda l: (l, 0, 0)),
            pl.BlockSpec((1, 1, 3 * D), lambda l: (l, 0, 0)),
        ],
        out_specs=pl.BlockSpec((1, 8, 3 * D), lambda l: (l, 0, 0)),
        out_shape=jax.ShapeDtypeStruct((DEPTH, 8, 3 * D), F32),
        compiler_params=pltpu.CompilerParams(
            dimension_semantics=("arbitrary",), vmem_limit_bytes=VMEM_LIMIT),
        name="modulation",
    )(cc, w_mod, b_mod.reshape(DEPTH, 1, 3 * D))


def _inproj_kernel(x_ref, modx_ref, modc_ref, g_ref, w_ref, u_ref, h_ref):
    @pl.when(pl.program_id(1) == 0)
    def _():
        modx = modx_ref[0]
        modc = modc_ref[0]
        for r in range(0, IN_TM, IN_PREP_ROWS):
            x = x_ref[r:r + IN_PREP_ROWS, :]
            y = x * lax.rsqrt(jnp.mean(x * x, axis=-1, keepdims=True) + EPS) * g_ref[...]
            row = ((pl.program_id(0) % IN_TILES_PER_BATCH) * IN_TM + r
                   + lax.broadcasted_iota(jnp.int32, x.shape, 0))
            is_ctx = row >= SEQ
            scale = jnp.where(is_ctx, modc[:, D:2 * D], modx[:, D:2 * D])
            shift = jnp.where(is_ctx, modc[:, 0:D], modx[:, 0:D])
            h = (y * (1.0 + scale) + shift).astype(BF16)
            h_ref[r:r + IN_PREP_ROWS, :] = h
            u_ref[r:r + IN_PREP_ROWS, :] = _dot_nt(h, w_ref[...]).astype(u_ref.dtype)

    @pl.when(pl.program_id(1) > 0)
    def _():
        u_ref[...] = _dot_nt(h_ref[...], w_ref[...]).astype(u_ref.dtype)


def _inproj(xs, mod_l, norm_g, w_in_p, layer):
    return pl.pallas_call(
        _inproj_kernel,
        grid=(N_ROWS // IN_TM, IN_PAD // IN_TN),
        in_specs=[
            pl.BlockSpec((IN_TM, D), lambda i, j: (i, 0)),
            pl.BlockSpec((1, 1, 3 * D), lambda i, j: (i // IN_TILES_PER_BATCH, 0, 0)),
            pl.BlockSpec((1, 1, 3 * D), lambda i, j: (CTX_GROUP, 0, 0)),
            pl.BlockSpec((1, D), lambda i, j: (0, 0)),
            pl.BlockSpec((None, IN_TN, D), lambda i, j: (layer, j, 0)),
        ],
        out_specs=pl.BlockSpec((IN_TM, IN_TN), lambda i, j: (i, j)),
        out_shape=jax.ShapeDtypeStruct((N_ROWS, IN_PAD), ACT),
        scratch_shapes=[pltpu.VMEM((IN_TM, D), BF16)],
        compiler_params=pltpu.CompilerParams(
            dimension_semantics=("arbitrary", "arbitrary"), vmem_limit_bytes=VMEM_LIMIT),
        name="inproj",
    )(xs, mod_l, mod_l, norm_g, w_in_p)


def _swap16(r):
    lane = lax.broadcasted_iota(jnp.int32, r.shape, 1)
    even = ((lane // 16) % 2) == 0
    return jnp.where(even, pltpu.roll(r, 112, axis=1), pltpu.roll(r, 16, axis=1))


def _mla_project(cqkv_ref, wq_ref, wk_ref, wvt_ref, gq_ref, gkv_ref):
    cqkv = cqkv_ref[...].astype(F32)
    cq = cqkv[:, :MLA_Q_RANK]
    ckv = cqkv[:, MLA_Q_RANK:]

    cq_n = cq * lax.rsqrt(jnp.mean(cq * cq, axis=-1, keepdims=True) + EPS) * gq_ref[...]
    q_all = _dot(cq_n.astype(BF16), wq_ref[...])
    ckv_n = (ckv * lax.rsqrt(jnp.mean(ckv * ckv, axis=-1, keepdims=True) + EPS) * gkv_ref[...]).astype(BF16)
    kn_all = _dot(ckv_n, wk_ref[...])
    vt_all = _dot_nt(wvt_ref[...], ckv_n)
    return q_all, kn_all, vt_all


def _mla_finish(q_all, kn_all, vt_all, small_ref, cos_ref, sin_ref, qtab_ref, qn_ref, kn_ref, q_ref, k_ref, vt_ref):
    small = small_ref[...].astype(F32)
    lane = lax.broadcasted_iota(jnp.int32, small.shape, 1)
    kr = jnp.where(lane < MLA_ROPE, small, 0.0)
    kr_ss = jnp.sum(kr * kr, axis=-1, keepdims=True)
    qn_g = qn_ref[...] * _SM_SCALE_LOG2E
    kn_g = kn_ref[...]
    krg = kr * kn_g[:, MLA_NOPE:MLA_QPAD]
    k_rot = krg * cos_ref[...] + _swap16(krg) * sin_ref[...]
    k_rot2 = k_rot + pltpu.roll(k_rot, MLA_ROPE, axis=1)
    qtab = qtab_ref[...] * qn_g[:, MLA_NOPE:MLA_QPAD]
    ones_rows = jnp.ones((VT_ROWS - MLA_DV, ROW_TILE), BF16)

    for h in range(MLA_H):
        q0 = q_all[:, h * MLA_QPAD:h * MLA_QPAD + MLA_NOPE]
        q1 = q_all[:, h * MLA_QPAD + MLA_NOPE:(h + 1) * MLA_QPAD]
        ss = jnp.sum(q0 * q0 + 0.5 * (q1 * q1), axis=-1, keepdims=True)
        inv = lax.rsqrt(ss * (1.0 / MLA_QK) + EPS)
        q_ref[h, :, 0:MLA_NOPE] = (q0 * inv * qn_g[:, 0:MLA_NOPE]).astype(BF16)
        q_ref[h, :, MLA_NOPE:MLA_QPAD] = (q1 * inv * qtab).astype(BF16)

        kn = kn_all[:, h * MLA_NOPE:(h + 1) * MLA_NOPE]
        inv = lax.rsqrt((jnp.sum(kn * kn, axis=-1, keepdims=True) + kr_ss) * (1.0 / MLA_QK) + EPS)
        k_ref[h, :, 0:MLA_NOPE] = (kn * inv * kn_g[:, 0:MLA_NOPE]).astype(BF16)
        k_ref[h, :, MLA_NOPE:MLA_QPAD] = (k_rot2 * inv).astype(BF16)
        vt_ref[h, 0:MLA_DV, :] = vt_all[h * MLA_DV:(h + 1) * MLA_DV, :].astype(BF16)
        vt_ref[h, MLA_DV:VT_ROWS, :] = ones_rows


def _attn_kernel(q_ref, k_ref, vt_ref, o_ref):
    n_kv = SEQ // KV_TILE
    lat_bounds = [(c * KV_TILE, (c + 1) * KV_TILE) for c in range(n_kv - 1)] + [((n_kv - 1) * KV_TILE, SB)]
    tiles = [((r0, r0 + Q_TILE), lat_bounds) for r0 in range(0, SEQ, Q_TILE)] + [((SEQ, SB), [(SEQ, SB)])]
    items = [(t, c) for t, (_, bounds) in enumerate(tiles) for c in range(len(bounds))]

    def scores(n):
        t, c = items[n]
        (q0, q1), bounds = tiles[t]
        lo, hi = bounds[c]
        return _dot_nt(k_ref[0, lo:hi, :], q_ref[0, q0:q1, :])

    ahead = [scores(n) for n in range(ATTN_LOOKAHEAD)]
    m = acc = None
    for n, (t, c) in enumerate(items):
        s = ahead.pop(0)
        if n + ATTN_LOOKAHEAD < len(items):
            ahead.append(scores(n + ATTN_LOOKAHEAD))
        (q0, q1), bounds = tiles[t]
        lo, hi = bounds[c]
        m_c = jnp.max(s, axis=0, keepdims=True)
        m_new = m_c if c == 0 else jnp.maximum(m, m_c)
        p = jnp.exp2(s - m_new)
        pv = _dot(vt_ref[0, :, lo:hi], p.astype(BF16))
        acc = pv if c == 0 else jnp.exp2(m - m_new) * acc + pv
        m = m_new
        if c == len(bounds) - 1:
            o_ref[q0:q1, :] = (acc[0:MLA_DV] / acc[MLA_DV:MLA_DV + 1]).T.astype(o_ref.dtype)


def _attention(q, k, vt):
    return pl.pallas_call(
        _attn_kernel,
        grid=(BATCH, MLA_H),
        in_specs=[
            pl.BlockSpec((1, SB, MLA_QPAD), lambda b, h: (h, b, 0)),
            pl.BlockSpec((1, SB, MLA_QPAD), lambda b, h: (h, b, 0)),
            pl.BlockSpec((1, VT_ROWS, SB), lambda b, h: (h, 0, b)),
        ],
        out_specs=pl.BlockSpec((SB, MLA_DV), lambda b, h: (b, h)),
        out_shape=jax.ShapeDtypeStruct((N_ROWS, MLA_H * MLA_DV), ACT),
        compiler_params=pltpu.CompilerParams(
            dimension_semantics=("arbitrary", "arbitrary"), vmem_limit_bytes=VMEM_LIMIT),
        name="mla_attention",
    )(q, k, vt)


def _log2_sigmoid(z):
    z2 = z * math.log2(math.e)
    return jnp.minimum(z2, 0.0) - jnp.log2(1.0 + jnp.exp2(-jnp.abs(z2)))


def _gla_kernel(qf_ref, kf_ref, vf_ref, sf_ref, qb_ref, kb_ref, vb_ref, sb_ref, wup_ref, bup_ref,
                cqkv_ref, cos_ref, sin_ref, qtab_ref, wq_ref, wk_ref, wvt_ref, gq_ref, gkv_ref, qn_ref, kn_ref,
                of_ref, ob_ref, mq_ref, mk_ref, mvt_ref, st_ref):
    @pl.when(pl.program_id(1) == 0)
    def _():
        st_ref[...] = jnp.zeros_like(st_ref)

    mla = _mla_project(cqkv_ref, wq_ref, wk_ref, wvt_ref, gq_ref, gkv_ref)

    hk = GLA_H * GLA_DK
    n_chunks = ROW_TILE // CHUNK
    row = lax.broadcasted_iota(jnp.int32, (ROW_TILE, ROW_TILE), 0)
    col = lax.broadcasted_iota(jnp.int32, (ROW_TILE, ROW_TILE), 1)
    same_chunk = (row // CHUNK) == (col // CHUNK)
    q_scale = GLA_DK ** -0.5

    dirs = ((qf_ref, kf_ref, vf_ref, sf_ref, of_ref), (qb_ref, kb_ref, vb_ref, sb_ref, ob_ref))
    heads = [(d, h) for d in range(2) for h in range(GLA_H)]
    ksl = lambda h: slice(h * GLA_DK, (h + 1) * GLA_DK)
    vsl = lambda h: slice(h * GLA_DV, (h + 1) * GLA_DV)
    csl = lambda c: slice(c * CHUNK, (c + 1) * CHUNK)
    keep = [jnp.logical_and(same_chunk, row >= col), jnp.logical_and(same_chunk, row <= col)]
    scan = [list(range(n_chunks)), list(range(n_chunks - 1, -1, -1))]

    z = [_dot(dirs[d][3][...], wup_ref[:, d * hk:(d + 1) * hk]) + bup_ref[:, d * hk:(d + 1) * hk] for d in range(2)]
    cs = []
    for d in range(2):
        la = _log2_sigmoid(z[d]) * (1.0 / GLA_TAU)
        la_hi = la.astype(BF16)
        la_lo = (la - la_hi.astype(F32)).astype(BF16)
        tri = jnp.where(keep[d], 1.0, 0.0).astype(BF16)
        cs.append(_dot(tri, jnp.concatenate([la_hi, la_lo], axis=1)))

    q_dec, k_inv, k_end, g = [], [], [], []
    for d in range(2):
        q_ref, k_ref = dirs[d][0], dirs[d][1]
        b = cs[d][:, :hk] + cs[d][:, hk:]
        edge = (lambda c: c * CHUNK + CHUNK - 1) if d == 0 else (lambda c: c * CHUNK)
        tots = [b[edge(c):edge(c) + 1] for c in range(n_chunks)]
        tot = jnp.concatenate([jnp.broadcast_to(t, (CHUNK, hk)) for t in tots], axis=0)
        kf = k_ref[...].astype(F32)
        q_dec.append(((q_ref[...].astype(F32) * q_scale) * jnp.exp2(b)).astype(BF16))
        k_inv.append((kf * jnp.exp2(-b)).astype(BF16))
        k_end.append(kf * jnp.exp2(tot - b))
        g.append([jnp.exp2(t) for t in tots])

    att = {(d, h): _dot_nt(q_dec[d][:, ksl(h)], k_inv[d][:, ksl(h)]) for (d, h) in heads}
    upd = {(d, h, c): _dot(k_end[d][csl(c), ksl(h)].T.astype(BF16), dirs[d][2][csl(c), vsl(h)])
           for (d, h) in heads for c in range(n_chunks)}
    o_intra = {(d, h): _dot(jnp.where(keep[d], att[d, h], 0.0).astype(BF16), dirs[d][2][:, vsl(h)])
               for (d, h) in heads}

    _mla_finish(*mla, sf_ref, cos_ref, sin_ref, qtab_ref, qn_ref, kn_ref, mq_ref, mk_ref, mvt_ref)

    states = {}
    for (d, h) in heads:
        st = st_ref[d, h]
        for c in scan[d]:
            states[d, h, c] = st.astype(BF16)
            g_col = jnp.broadcast_to(g[d][c][:, ksl(h)], (GLA_DK, GLA_DK)).T
            st = st * jnp.concatenate([g_col] * (GLA_DV // GLA_DK), axis=1) + upd[d, h, c]
        st_ref[d, h] = st

    for (d, h) in heads:
        o_ref = dirs[d][4]
        for c in range(n_chunks):
            o_inter = _dot(q_dec[d][csl(c), ksl(h)], states[d, h, c])
            o_ref[csl(c), vsl(h)] = (o_intra[d, h][csl(c)] + o_inter).astype(o_ref.dtype)


def _gla_and_mla_prep(u, wup_p, bup_p, cos_t, sin_t, qtab_t, wq_p, wk, wvt, gq, gkv, qn_p, kn_p):
    def fwd_pos(s):
        return jnp.where(s == 0, X_TILES, s - 1)

    def fwd_tile(b, s):
        return b * TILES_PER_BATCH + fwd_pos(s)

    def bwd_tile(b, s):
        return b * TILES_PER_BATCH + jnp.where(s == 0, X_TILES, X_TILES - s)

    def specs(tile):
        return [
            pl.BlockSpec((ROW_TILE, 512), lambda b, s: (tile(b, s), COL_BQ)),
            pl.BlockSpec((ROW_TILE, 512), lambda b, s: (tile(b, s), COL_BK)),
            pl.BlockSpec((ROW_TILE, 1024), lambda b, s: (tile(b, s), COL_BV)),
            pl.BlockSpec((ROW_TILE, 128), lambda b, s: (tile(b, s), COL_SMALL)),
        ]

    const = lambda b, s: (0, 0)
    table = pl.BlockSpec((ROW_TILE, 128), lambda b, s: (fwd_pos(s), 0))
    return pl.pallas_call(
        _gla_kernel,
        grid=(BATCH, TILES_PER_BATCH),
        in_specs=specs(fwd_tile) + specs(bwd_tile) + [
            pl.BlockSpec((128, 2 * GLA_H * GLA_DK), const),
            pl.BlockSpec((1, 2 * GLA_H * GLA_DK), const),
            pl.BlockSpec((ROW_TILE, 512), lambda b, s: (fwd_tile(b, s), COL_CQKV)),
            table, table, table,
            pl.BlockSpec((MLA_Q_RANK, MLA_H * MLA_QPAD), const),
            pl.BlockSpec((MLA_KV_RANK, MLA_H * MLA_NOPE), const),
            pl.BlockSpec((MLA_H * MLA_DV, MLA_KV_RANK), const),
            pl.BlockSpec((1, MLA_Q_RANK), const),
            pl.BlockSpec((1, MLA_KV_RANK), const),
            pl.BlockSpec((1, MLA_QPAD), const),
            pl.BlockSpec((1, MLA_QPAD), const),
        ],
        out_specs=[
            pl.BlockSpec((ROW_TILE, GLA_H * GLA_DV), lambda b, s: (fwd_tile(b, s), 0)),
            pl.BlockSpec((ROW_TILE, GLA_H * GLA_DV), lambda b, s: (bwd_tile(b, s), 0)),
            pl.BlockSpec((MLA_H, ROW_TILE, MLA_QPAD), lambda b, s: (0, fwd_tile(b, s), 0)),
            pl.BlockSpec((MLA_H, ROW_TILE, MLA_QPAD), lambda b, s: (0, fwd_tile(b, s), 0)),
            pl.BlockSpec((MLA_H, VT_ROWS, ROW_TILE), lambda b, s: (0, 0, fwd_tile(b, s))),
        ],
        out_shape=[
            jax.ShapeDtypeStruct((N_ROWS, GLA_H * GLA_DV), ACT),
            jax.ShapeDtypeStruct((N_ROWS, GLA_H * GLA_DV), ACT),
            jax.ShapeDtypeStruct((MLA_H, N_ROWS, MLA_QPAD), BF16),
            jax.ShapeDtypeStruct((MLA_H, N_ROWS, MLA_QPAD), BF16),
            jax.ShapeDtypeStruct((MLA_H, VT_ROWS, N_ROWS), BF16),
        ],
        scratch_shapes=[pltpu.VMEM((2, GLA_H, GLA_DK, GLA_DV), F32)],
        compiler_params=pltpu.CompilerParams(
            dimension_semantics=("arbitrary", "arbitrary"), vmem_limit_bytes=VMEM_LIMIT),
        name="gla_bidir_mla_prep",
    )(u, u, u, u, u, u, u, u, wup_p, bup_p, u, cos_t, sin_t, qtab_t, wq_p, wk, wvt, gq, gkv, qn_p, kn_p)


def _merge_kernel(av_ref, ab_ref, ac_ref, az_ref, bz_ref, cz_ref, ga_ref, gb_ref, gc_ref,
                  avp_ref, acp_ref, avn_ref, acn_ref, of_ref, ob_ref, oc_ref, x_ref, mod_ref,
                  cw_ref, gng_ref, wa_ref, wb_ref, wc_ref, wo_ref, out_ref, *, tiles_per_batch):
    tt = pl.program_id(0) % tiles_per_batch
    prev_ok = jnp.logical_and(tt != 0, tt < X_TILES).astype(F32)
    next_ok = (tt < X_TILES - 1).astype(F32)

    f32 = lambda ref: ref[...].astype(F32)
    p = f32(ac_ref) * f32(av_ref)
    p_prev = f32(acp_ref)[HALO - 1:HALO, :] * f32(avp_ref)[HALO - 1:HALO, :] * prev_ok
    p_next = f32(acn_ref)[0:1, :] * f32(avn_ref)[0:1, :] * next_ok
    rows = lax.broadcasted_iota(jnp.int32, p.shape, 0)
    p_up = jnp.where(rows == 0, p_prev, pltpu.roll(p, 1, axis=0))
    p_dn = jnp.where(rows == ROW_TILE - 1, p_next, pltpu.roll(p, ROW_TILE - 1, axis=0))
    cw = cw_ref[...]
    conv = p_up * cw[0:1] + p * cw[1:2] + p_dn * cw[2:3]
    y_a = f32(ab_ref) * conv * _silu(f32(az_ref))
    m = _sigmoid(f32(ga_ref)) * _dot(y_a.astype(BF16), wa_ref[...])

    gng = gng_ref[...]
    y_b = []
    for h in range(GLA_H):
        sl = slice(h * GLA_DV, (h + 1) * GLA_DV)
        oh = of_ref[:, sl].astype(F32) + ob_ref[:, sl].astype(F32)
        yh = oh * lax.rsqrt(jnp.mean(oh * oh, axis=-1, keepdims=True) + EPS) * gng[:, sl]
        y_b.append(yh * _silu(bz_ref[:, sl].astype(F32)))
    y_b = jnp.concatenate(y_b, axis=-1)
    m = m + _sigmoid(f32(gb_ref)) * _dot(y_b.astype(BF16), wb_ref[...])

    y_c = f32(oc_ref) * _silu(f32(cz_ref))
    m = m + _sigmoid(f32(gc_ref)) * _dot(y_c.astype(BF16), wc_ref[...])

    gate = mod_ref[0][:, 2 * D:3 * D]
    out_ref[...] = x_ref[...] + gate * _dot(m.astype(BF16), wo_ref[...])


def _merge(u, o_f, o_b, o_c, xs, mod_l, conv_w, gla_norm_g, wa, wb, wc, wo, layer, last):
    halo_blocks = ROW_TILE // HALO
    n_halo = N_ROWS // HALO
    if last:
        grid = (BATCH * X_TILES,)
        tile = lambda i: (i // X_TILES) * TILES_PER_BATCH + i % X_TILES
        out_rows = BATCH * SEQ
        out_map = lambda i: (i, 0)
    else:
        grid = (N_TILES,)
        tile = lambda i: i
        out_rows = N_ROWS
        out_map = lambda i: (i, 0)

    def col(c):
        return pl.BlockSpec((ROW_TILE, D), lambda i: (tile(i), c))

    def prev_rows(c):
        return pl.BlockSpec((HALO, D), lambda i: (jnp.maximum(tile(i) * halo_blocks - 1, 0), c))

    def next_rows(c):
        return pl.BlockSpec((HALO, D), lambda i: (jnp.minimum((tile(i) + 1) * halo_blocks, n_halo - 1), c))

    row_block = pl.BlockSpec((ROW_TILE, D), lambda i: (tile(i), 0))
    const = lambda i: (0, 0)
    weight = pl.BlockSpec((None, D, D), lambda i: (layer, 0, 0))

    return pl.pallas_call(
        functools.partial(_merge_kernel, tiles_per_batch=X_TILES if last else TILES_PER_BATCH),
        grid=grid,
        in_specs=[col(COL_AV), col(COL_AB), col(COL_AC), col(COL_AZ), col(COL_BZ), col(COL_CZ),
                  col(COL_GA), col(COL_GB), col(COL_GC),
                  prev_rows(COL_AV), prev_rows(COL_AC), next_rows(COL_AV), next_rows(COL_AC),
                  row_block, row_block, row_block, row_block,
                  pl.BlockSpec((1, 1, 3 * D), lambda i: (_tile_group(tile(i)), 0, 0)),
                  pl.BlockSpec((3, D), const), pl.BlockSpec((1, D), const),
                  weight, weight, weight, weight],
        out_specs=pl.BlockSpec((ROW_TILE, D), out_map),
        out_shape=jax.ShapeDtypeStruct((out_rows, D), F32),
        compiler_params=pltpu.CompilerParams(
            dimension_semantics=("arbitrary",), vmem_limit_bytes=VMEM_LIMIT),
        name="merge_last" if last else "merge",
    )(u, u, u, u, u, u, u, u, u, u, u, u, u, o_f, o_b, o_c, xs, mod_l, conv_w, gla_norm_g,
      wa, wb, wc, wo)


def _rope_tables():
    t = np.arange(SEQ)
    row = (t // GRID_W).astype(np.float32)
    colp = (t % GRID_W).astype(np.float32)
    n_freq = MLA_ROPE // 4
    freqs = (np.float32(ROPE_BASE) ** (-np.arange(n_freq, dtype=np.float32) / np.float32(n_freq))).astype(np.float32)
    ang_r = row[:, None] * freqs[None, :]
    ang_c = colp[:, None] * freqs[None, :]
    ang = np.concatenate([ang_r, ang_r, ang_c, ang_c], axis=-1).astype(np.float32)
    cos = np.ones((SB, 128), np.float32)
    sin = np.zeros((SB, 128), np.float32)
    cos[:SEQ, :MLA_ROPE] = np.cos(ang)
    sign = np.where((np.arange(MLA_ROPE) // n_freq) % 2 == 0, -1.0, 1.0).astype(np.float32)
    sin[:SEQ, :MLA_ROPE] = np.sin(ang) * sign[None, :]
    qtab = np.concatenate([cos[:, :MLA_ROPE], sin[:, :MLA_ROPE]], axis=1)
    return jnp.asarray(cos), jnp.asarray(sin), jnp.asarray(qtab)


def _swap16_index():
    l = np.arange(MLA_ROPE)
    return np.where((l // 16) % 2 == 0, l + 16, l - 16)


WP = 1024
W_SPLIT = 7776 - IN_HEAD_COLS
W_SMALL = 7200 - IN_HEAD_COLS
N_WBLOCKS = IN_PAD // WP
HEAD_WBLOCKS = IN_HEAD_COLS // WP


def _permute_w_in_kernel(a_ref, b_ref, o_ref):
    blk = pl.program_id(1)

    @pl.when(blk < HEAD_WBLOCKS)
    def _():
        o_ref[0] = a_ref[0].astype(BF16)

    @pl.when(jnp.logical_and(blk >= HEAD_WBLOCKS, blk < N_WBLOCKS - 1))
    def _():
        o_ref[0, 0:WP - W_SPLIT, :] = a_ref[0, W_SPLIT:WP, :].astype(BF16)
        o_ref[0, WP - W_SPLIT:WP, :] = b_ref[0, 0:W_SPLIT, :].astype(BF16)

    @pl.when(blk == N_WBLOCKS - 1)
    def _():
        o_ref[0, 0:W_SPLIT - W_SMALL, :] = a_ref[0, W_SMALL:W_SPLIT, :].astype(BF16)
        o_ref[0, W_SPLIT - W_SMALL:W_SPLIT, :] = a_ref[0, 0:W_SMALL, :].astype(BF16)
        o_ref[0, W_SPLIT:WP, :] = jnp.zeros((WP - W_SPLIT, D), BF16)


def _permute_w_in(w_in):
    wt = jnp.transpose(w_in, (0, 2, 1))
    last = N_WBLOCKS - 1
    a_map = lambda l, b: (l, jnp.where(b == last, HEAD_WBLOCKS, b), 0)
    b_map = lambda l, b: (l, jnp.clip(b + 1, HEAD_WBLOCKS, last), 0)
    return pl.pallas_call(
        _permute_w_in_kernel,
        grid=(DEPTH, N_WBLOCKS),
        in_specs=[pl.BlockSpec((1, WP, D), a_map), pl.BlockSpec((1, WP, D), b_map)],
        out_specs=pl.BlockSpec((1, WP, D), lambda l, b: (l, b, 0)),
        out_shape=jax.ShapeDtypeStruct((DEPTH, IN_PAD, D), BF16),
        compiler_params=pltpu.CompilerParams(
            dimension_semantics=("arbitrary", "arbitrary"), vmem_limit_bytes=VMEM_LIMIT),
        name="permute_w_in",
    )(wt, wt)


def kernel(x, c, ctx, c_ctx, w_mod, b_mod, norm_g, w_in, conv_w, gla_wa_up_f, gla_ba_f, gla_wa_up_b,
           gla_ba_b, gla_norm_g, mla_q_norm_g, mla_kv_norm_g, mla_wq_up, mla_wkv_up, mla_qn_g, mla_kn_g,
           w_br_a, w_br_b, w_br_c, w_out):
    cos_t, sin_t, qtab_t = _rope_tables()
    perm = _swap16_index()
    xs = jnp.concatenate([x, ctx], axis=1).reshape(N_ROWS, D)
    cc = jnp.concatenate([c, c_ctx[None, :], jnp.zeros((8 - BATCH - 1, D), F32)], axis=0)
    mods = _modulation(cc, w_mod, b_mod).reshape(DEPTH, 8, 1, 3 * D)

    hk = GLA_H * GLA_DK
    w_in_p = _permute_w_in(w_in)
    wup_p = jnp.zeros((DEPTH, 128, 2 * hk), F32)
    wup_p = wup_p.at[:, SMALL_AF:SMALL_AF + GLA_RANK, :hk].set(gla_wa_up_f)
    wup_p = wup_p.at[:, SMALL_AB:SMALL_AB + GLA_RANK, hk:].set(gla_wa_up_b).astype(BF16)
    bup_p = jnp.concatenate([gla_ba_f, gla_ba_b], axis=1)[:, None, :]
    wq4 = mla_wq_up.reshape(DEPTH, MLA_Q_RANK, MLA_H, MLA_QK)
    wq_p = jnp.concatenate([wq4, wq4[:, :, :, MLA_NOPE:][:, :, :, perm]], axis=3)
    wq_p = wq_p.reshape(DEPTH, MLA_Q_RANK, MLA_H * MLA_QPAD).astype(BF16)
    qn_p = jnp.concatenate([mla_qn_g, mla_qn_g[:, MLA_NOPE:][:, perm]], axis=1)[:, None, :]
    kn_p = jnp.pad(mla_kn_g, ((0, 0), (0, MLA_QPAD - MLA_QK)))[:, None, :]
    wkv4 = mla_wkv_up.reshape(DEPTH, MLA_KV_RANK, MLA_H, MLA_NOPE + MLA_DV)
    wk = wkv4[:, :, :, :MLA_NOPE].reshape(DEPTH, MLA_KV_RANK, MLA_H * MLA_NOPE).astype(BF16)
    wvt = wkv4[:, :, :, MLA_NOPE:].reshape(DEPTH, MLA_KV_RANK, MLA_H * MLA_DV).transpose(0, 2, 1).astype(BF16)
    wa, wb, wc, wo = (w.astype(BF16) for w in (w_br_a, w_br_b, w_br_c, w_out))

    for l in range(DEPTH):
        last = l == DEPTH - 1
        u = _inproj(xs, mods[l], norm_g[l][None, :], w_in_p, l)
        o_f, o_b, q, k, vt = _gla_and_mla_prep(
            u, wup_p[l], bup_p[l], cos_t, sin_t, qtab_t, wq_p[l], wk[l], wvt[l],
            mla_q_norm_g[l][None, :], mla_kv_norm_g[l][None, :], qn_p[l], kn_p[l])
        o_c = _attention(q, k, vt)
        xs = _merge(u, o_f, o_b, o_c, xs, mods[l], conv_w[l], gla_norm_g[l][None, :],
                    wa, wb, wc, wo, l, last)
    return xs.reshape(BATCH, SEQ, D)
```
